```python
import math
import jax, jax.numpy as jnp
from jax import lax
import numpy as np

D_MODEL = 2048
BATCH = 2
SEQ = 8192
DEPTH = 4

CHUNK = 128
NORM_EPS = 1e-6
RET_HEADS = 4
RET_HEAD_DIM = 128
RET_WIDTH = RET_HEADS * RET_HEAD_DIM
RET_GN_EPS = 1e-6
ROPE_BASE = 10000.0
SSM_HEADS = 16
SSM_HEAD_DIM = 64
SSM_WIDTH = SSM_HEADS * SSM_HEAD_DIM
SSM_GROUPS = 2
SSM_STATE = 128
SSM_CONV = 4
SSM_CONV_DIM = SSM_WIDTH + 2 * SSM_GROUPS * SSM_STATE
SSM_NORM_EPS = 1e-5
RWKV_HEADS = 8
RWKV_HEAD_DIM = 64
RWKV_WIDTH = RWKV_HEADS * RWKV_HEAD_DIM
RWKV_DECAY_RANK = 96
RWKV_AAA_RANK = 96
RWKV_MV_RANK = 64
RWKV_GATE_RANK = 256
RWKV_LN_EPS = 64e-5
D_MIX = RET_WIDTH + SSM_WIDTH + RWKV_WIDTH
RET_COLS = 4 * RET_WIDTH
SSM_COLS = SSM_WIDTH + SSM_CONV_DIM + SSM_HEADS
RWKV_COLS = 3 * RWKV_WIDTH + RWKV_DECAY_RANK + RWKV_AAA_RANK + RWKV_GATE_RANK
SSM_OFF = RET_COLS
RWKV_OFF = RET_COLS + SSM_COLS
IN_COLS = RET_COLS + SSM_COLS + RWKV_COLS
D_FF = ((8 * D_MODEL + 3 * 256 - 1) // (3 * 256)) * 256

kernel_name = 'hymba_style_retention_rwkv7_mamba2_hybrid'


def rms_norm(x, w, eps=NORM_EPS):
    xf = x.astype(jnp.float32)
    y = xf * lax.rsqrt(jnp.mean(xf * xf, axis=-1, keepdims=True) + eps)
    return (y * w.astype(jnp.float32)).astype(x.dtype)


def head_norm(y, eps):
    yc = y - jnp.mean(y, axis=-1, keepdims=True)
    return yc * lax.rsqrt(jnp.mean(yc * yc, axis=-1, keepdims=True) + eps)


def token_shift(x):
    return jnp.pad(x, ((0, 0), (1, 0), (0, 0)))[:, :-1]


def rotary(x, pos):
    half = x.shape[-1] // 2
    inv_freq = ROPE_BASE ** (-jnp.arange(half, dtype=jnp.float32) / half)
    ang = pos.astype(jnp.float32)[:, None] * inv_freq[None, :]
    cos = jnp.cos(ang)[None, :, None, :]
    sin = jnp.sin(ang)[None, :, None, :]
    x1, x2 = x[..., :half], x[..., half:]
    return jnp.concatenate([x1 * cos - x2 * sin, x1 * sin + x2 * cos], axis=-1)


def chunk_state_scan(contrib, decay):
    def step(s, inp):
        c, d = inp
        return s * d + c, s
    _, prev = lax.scan(step, jnp.zeros_like(contrib[0]), (contrib, decay))
    return prev


def retention_mixer(q, k, v, g):
    Bsz, S, _ = q.shape
    H, Dh, C = RET_HEADS, RET_HEAD_DIM, CHUNK
    N = S // C
    shp = (Bsz, S, H, Dh)
    pos = jnp.arange(S)
    q = rotary(q.reshape(shp), pos)
    k = rotary(k.reshape(shp), pos) * (Dh ** -0.5)
    v = v.reshape(shp)
    log_gamma = jnp.log1p(-jnp.exp2(-5.0 - jnp.arange(H, dtype=jnp.float32)))
    idx = jnp.arange(C, dtype=jnp.float32)
    rel = idx[:, None] - idx[None, :]
    decay_in = jnp.where(rel >= 0, jnp.exp(jnp.maximum(rel, 0.0)[None] * log_gamma[:, None, None]), 0.0)
    qc, kc, vc = (t.reshape(Bsz, N, C, H, Dh) for t in (q, k, v))
    scores = jnp.einsum('bnihd,bnjhd->bnhij', qc, kc) * decay_in
    inner = jnp.einsum('bnhij,bnjhe->bnihe', scores, vc)
    zeta = jnp.exp((C - 1.0 - idx)[None, :] * log_gamma[:, None])
    contrib = jnp.einsum('bnjhd,hj,bnjhe->nbhde', kc, zeta, vc)
    chunk_decay = jnp.broadcast_to(jnp.exp(C * log_gamma)[None, None, :, None, None], (N, 1, H, 1, 1))
    prev = chunk_state_scan(contrib, chunk_decay)
    xi = jnp.exp((idx + 1.0)[None, :] * log_gamma[:, None])
    cross = jnp.einsum('bnihd,hi,nbhde->bnihe', qc, xi, prev)
    y = head_norm((inner + cross).reshape(shp), RET_GN_EPS)
    return y.reshape(Bsz, S, RET_WIDTH) * jax.nn.silu(g)


def ssd_chunked(xs, dt, A, Bm, Cm):
    Bsz, S, H, P = xs.shape
    G, Nst, L = SSM_GROUPS, SSM_STATE, CHUNK
    J = H // G
    Nc = S // L
    x = (xs * dt[..., None]).reshape(Bsz, Nc, L, G, J, P)
    a = (dt * A).reshape(Bsz, Nc, L, G, J).transpose(0, 3, 4, 1, 2)
    Bc = Bm.reshape(Bsz, Nc, L, G, Nst)
    Cc = Cm.reshape(Bsz, Nc, L, G, Nst)
    a_cum = jnp.cumsum(a, axis=-1)
    seg = a_cum[..., :, None] - a_cum[..., None, :]
    causal = jnp.tril(jnp.ones((L, L), dtype=bool))
    decay_in = jnp.exp(jnp.where(causal, seg, -jnp.inf))
    cb = jnp.einsum('bclgn,bcsgn->bgcls', Cc, Bc)
    y_diag = jnp.einsum('bgjcls,bcsgjp->bclgjp', cb[:, :, None] * decay_in, x)
    decay_to_end = jnp.exp(a_cum[..., -1:] - a_cum).transpose(0, 3, 4, 1, 2)
    states = jnp.einsum('bclgn,bclgjp->cbgjpn', Bc, x * decay_to_end[..., None])
    chunk_decay = jnp.exp(a_cum[..., -1]).transpose(3, 0, 1, 2)[..., None, None]
    prev = chunk_state_scan(states, chunk_decay)
    decay_from_start = jnp.exp(a_cum).transpose(0, 3, 4, 1, 2)
    y_off = jnp.einsum('bclgn,cbgjpn->bclgjp', Cc, prev) * decay_from_start[..., None]
    return (y_diag + y_off).reshape(Bsz, S, H, P)


def mamba2_mixer(z, xbc, dt_raw, conv_w, conv_b, dt_bias, a_log, d_skip, norm_w):
    Bsz, S, _ = z.shape
    GN = SSM_GROUPS * SSM_STATE
    xbc = lax.conv_general_dilated(
        xbc, conv_w.astype(xbc.dtype)[:, None, :], window_strides=(1,),
        padding=((SSM_CONV - 1, 0),), dimension_numbers=('NWC', 'WIO', 'NWC'),
        feature_group_count=SSM_CONV_DIM) + conv_b
    xbc = jax.nn.silu(xbc)
    xs = xbc[..., :SSM_WIDTH].reshape(Bsz, S, SSM_HEADS, SSM_HEAD_DIM)
    Bm = xbc[..., SSM_WIDTH:SSM_WIDTH + GN].reshape(Bsz, S, SSM_GROUPS, SSM_STATE)
    Cm = xbc[..., SSM_WIDTH + GN:].reshape(Bsz, S, SSM_GROUPS, SSM_STATE)
    dt = jax.nn.softplus(dt_raw + dt_bias)
    A = -jnp.exp(a_log.astype(jnp.float32))
    y = ssd_chunked(xs, dt, A, Bm, Cm) + xs * d_skip[:, None]
    y = (y.reshape(Bsz, S, SSM_WIDTH) * jax.nn.silu(z)).reshape(Bsz, S, SSM_GROUPS, SSM_WIDTH // SSM_GROUPS)
    y = y * lax.rsqrt(jnp.mean(y * y, axis=-1, keepdims=True) + SSM_NORM_EPS)
    return y.reshape(Bsz, S, SSM_WIDTH) * norm_w


def rwkv7_scan(r, w, k, v, kk, a):
    def step(state, inp):
        r_t, w_t, k_t, v_t, kk_t, a_t = inp
        sa = jnp.einsum('bhvk,bhk->bhv', state, -kk_t)
        state = (state * w_t[:, :, None, :] + sa[..., None] * (kk_t * a_t)[:, :, None, :]
                 + v_t[..., None] * k_t[:, :, None, :])
        return state, jnp.einsum('bhvk,bhk->bhv', state, r_t)
    Bsz, _, H, D = r.shape
    xs = tuple(jnp.moveaxis(t, 1, 0) for t in (r, w, k, v, kk, a))
    s0 = jnp.zeros((Bsz, H, D, D), dtype=r.dtype)
    _, y = lax.scan(step, s0, xs)
    return jnp.moveaxis(y, 0, 1)


def rwkv7_mixer(p, mu, w0, w2, a0, a2, g2, k_k, k_a, r_k, ln_w, ln_b, v_res=None):
    Bsz, S, _ = p.shape
    W = RWKV_WIDTH
    hd = (Bsz, S, RWKV_HEADS, RWKV_HEAD_DIM)
    p = p + (token_shift(p) - p) * mu
    r, k, v = p[..., :W], p[..., W:2 * W], p[..., 2 * W:3 * W]
    o1 = 3 * W + RWKV_DECAY_RANK
    o2 = o1 + RWKV_AAA_RANK
    xw, xa, xg = p[..., 3 * W:o1], p[..., o1:o2], p[..., o2:]
    w_log = -jax.nn.softplus(-(w0 + jnp.tanh(xw) @ w2)) - 0.5
    decay = jnp.exp(-jnp.exp(w_log))
    if v_res is not None:
        pv, mu_v, v0, v2, v_first = v_res
        pv = pv + (token_shift(pv) - pv) * mu_v
        v = v + (v_first - v) * jax.nn.sigmoid(v0 + pv @ v2)
    a = jax.nn.sigmoid(a0 + xa @ a2)
    g = jax.nn.sigmoid(xg) @ g2
    kk = (k * k_k).reshape(hd)
    kk = kk / jnp.maximum(jnp.sqrt(jnp.sum(kk * kk, axis=-1, keepdims=True)), 1e-12)
    k = k * (1.0 + (a - 1.0) * k_a)
    rh, kh, vh, ah, wh = (t.reshape(hd) for t in (r, k, v, a, decay))
    y = rwkv7_scan(rh, wh, kh, vh, kk, ah)
    y = head_norm(y, RWKV_LN_EPS).reshape(Bsz, S, W) * ln_w + ln_b
    bonus = jnp.sum(rh * kh * r_k, axis=-1, keepdims=True) * vh
    return (y + bonus.reshape(Bsz, S, W)) * g, v


def setup_inputs(seed: int = 0) -> dict:
    key = jax.random.key(seed)
    ks = iter(jax.random.split(key, 48))
    f32 = jnp.float32
    L, Lv, D = DEPTH, DEPTH - 1, D_MODEL

    def nrm(shape, scale):
        return jax.random.normal(next(ks), shape, f32) * scale

    def unif(shape, lo, hi):
        return jax.random.uniform(next(ks), shape, f32, lo, hi)

    x = nrm((BATCH, SEQ, D), 1.0)
    norm_mix_w = 1.0 + nrm((L, D), 0.02)
    w_in_first = nrm((D, IN_COLS), D ** -0.5)
    w_in_rest = nrm((Lv, D, IN_COLS + RWKV_MV_RANK), D ** -0.5)
    ssm_conv_w = nrm((L, SSM_CONV, SSM_CONV_DIM), SSM_CONV ** -0.5)
    ssm_conv_b = nrm((L, SSM_CONV_DIM), 0.02)
    dt0 = jnp.exp(unif((L, SSM_HEADS), math.log(1e-3), math.log(1e-1)))
    ssm_dt_bias = dt0 + jnp.log(-jnp.expm1(-dt0))
    ssm_a_log = jnp.log(unif((L, SSM_HEADS), 1.0, 16.0))
    ssm_d = 1.0 + nrm((L, SSM_HEADS), 0.02)
    ssm_norm_w = 1.0 + nrm((L, SSM_WIDTH), 0.02)
    rwkv_mu = unif((L, RWKV_COLS), 0.0, 1.0)
    rwkv_mu_v = unif((Lv, RWKV_MV_RANK), 0.0, 1.0)
    ramp = jnp.arange(RWKV_WIDTH, dtype=f32) / (RWKV_WIDTH - 1)
    rwkv_w0 = -6.5 + 5.0 * ramp ** 0.9 + nrm((L, RWKV_WIDTH), 0.1)
    rwkv_w2 = nrm((L, RWKV_DECAY_RANK, RWKV_WIDTH), 0.5 * RWKV_DECAY_RANK ** -0.5)
    rwkv_a0 = nrm((L, RWKV_WIDTH), 0.1)
    rwkv_a2 = nrm((L, RWKV_AAA_RANK, RWKV_WIDTH), 0.5 * RWKV_AAA_RANK ** -0.5)
    rwkv_v0 = 1.0 + nrm((Lv, RWKV_WIDTH), 0.1)
    rwkv_v2 = nrm((Lv, RWKV_MV_RANK, RWKV_WIDTH), 0.5 * RWKV_MV_RANK ** -0.5)
    rwkv_g2 = nrm((L, RWKV_GATE_RANK, RWKV_WIDTH), RWKV_GATE_RANK ** -0.5)
    rwkv_k_k = 0.85 + nrm((L, RWKV_WIDTH), 0.02)
    rwkv_k_a = 1.0 + nrm((L, RWKV_WIDTH), 0.02)
    rwkv_r_k = nrm((L, RWKV_HEADS, RWKV_HEAD_DIM), 0.1)
    rwkv_ln_w = 1.0 + nrm((L, RWKV_WIDTH), 0.02)
    rwkv_ln_b = nrm((L, RWKV_WIDTH), 0.02)
    w_out = nrm((L, D_MIX, D), D_MIX ** -0.5)
    norm_ffn_w = 1.0 + nrm((L, D), 0.02)
    ffn_w_gate = nrm((L, D, D_FF), D ** -0.5)
    ffn_w_up = nrm((L, D, D_FF), D ** -0.5)
    ffn_w_down = nrm((L, D_FF, D), D_FF ** -0.5)
    final_norm_w = 1.0 + nrm((D,), 0.02)
    return {
        'x': x, 'norm_mix_w': norm_mix_w, 'w_in_first': w_in_first, 'w_in_rest': w_in_rest,
        'ssm_conv_w': ssm_conv_w, 'ssm_conv_b': ssm_conv_b, 'ssm_dt_bias': ssm_dt_bias,
        'ssm_a_log': ssm_a_log, 'ssm_d': ssm_d, 'ssm_norm_w': ssm_norm_w,
        'rwkv_mu': rwkv_mu, 'rwkv_mu_v': rwkv_mu_v, 'rwkv_w0': rwkv_w0, 'rwkv_w2': rwkv_w2,
        'rwkv_a0': rwkv_a0, 'rwkv_a2': rwkv_a2, 'rwkv_v0': rwkv_v0, 'rwkv_v2': rwkv_v2,
        'rwkv_g2': rwkv_g2, 'rwkv_k_k': rwkv_k_k, 'rwkv_k_a': rwkv_k_a, 'rwkv_r_k': rwkv_r_k,
        'rwkv_ln_w': rwkv_ln_w, 'rwkv_ln_b': rwkv_ln_b, 'w_out': w_out,
        'norm_ffn_w': norm_ffn_w, 'ffn_w_gate': ffn_w_gate, 'ffn_w_up': ffn_w_up,
        'ffn_w_down': ffn_w_down, 'final_norm_w': final_norm_w,
    }


def reference(x, norm_mix_w, w_in_first, w_in_rest, ssm_conv_w, ssm_conv_b, ssm_dt_bias,
              ssm_a_log, ssm_d, ssm_norm_w, rwkv_mu, rwkv_mu_v, rwkv_w0, rwkv_w2,
              rwkv_a0, rwkv_a2, rwkv_v0, rwkv_v2, rwkv_g2, rwkv_k_k, rwkv_k_a, rwkv_r_k,
              rwkv_ln_w, rwkv_ln_b, w_out, norm_ffn_w, ffn_w_gate, ffn_w_up, ffn_w_down,
              final_norm_w):
    v_first = None
    for l in range(DEPTH):
        h = rms_norm(x, norm_mix_w[l])
        w_in = w_in_first if l == 0 else w_in_rest[l - 1]
        proj = (h @ w_in).astype(jnp.float32)
        pr = proj[..., :RET_COLS]
        y_ret = retention_mixer(pr[..., :RET_WIDTH], pr[..., RET_WIDTH:2 * RET_WIDTH],
                                pr[..., 2 * RET_WIDTH:3 * RET_WIDTH], pr[..., 3 * RET_WIDTH:])
        ps = proj[..., SSM_OFF:SSM_OFF + SSM_COLS]
        y_ssm = mamba2_mixer(ps[..., :SSM_WIDTH], ps[..., SSM_WIDTH:SSM_WIDTH + SSM_CONV_DIM],
                             ps[..., SSM_WIDTH + SSM_CONV_DIM:], ssm_conv_w[l], ssm_conv_b[l],
                             ssm_dt_bias[l], ssm_a_log[l], ssm_d[l], ssm_norm_w[l])
        pw = proj[..., RWKV_OFF:RWKV_OFF + RWKV_COLS]
        v_res = None if l == 0 else (proj[..., IN_COLS:], rwkv_mu_v[l - 1], rwkv_v0[l - 1],
                                     rwkv_v2[l - 1], v_first)
        y_rwkv, v_l = rwkv7_mixer(pw, rwkv_mu[l], rwkv_w0[l], rwkv_w2[l], rwkv_a0[l], rwkv_a2[l],
                                  rwkv_g2[l], rwkv_k_k[l], rwkv_k_a[l], rwkv_r_k[l],
                                  rwkv_ln_w[l], rwkv_ln_b[l], v_res)
        if l == 0:
            v_first = v_l
        y = jnp.concatenate([y_ret, y_ssm, y_rwkv], axis=-1).astype(x.dtype)
        x = x + y @ w_out[l]
        h = rms_norm(x, norm_ffn_w[l])
        x = x + (jax.nn.silu(h @ ffn_w_gate[l]) * (h @ ffn_w_up[l])) @ ffn_w_down[l]
    return rms_norm(x, final_norm_w)
```

```python
import functools
import math

import jax
import jax.numpy as jnp
from jax import lax
from jax.experimental import pallas as pl
from jax.experimental.pallas import tpu as pltpu

F32 = jnp.float32
BF16 = jnp.bfloat16

D_MODEL = 2048
NORM_EPS = 1e-6
RET_HEADS, RET_HEAD_DIM = 4, 128
RET_WIDTH = RET_HEADS * RET_HEAD_DIM
RET_GN_EPS = 1e-6
ROPE_BASE = 10000.0
RET_CHUNK = 128
SSM_HEADS, SSM_HEAD_DIM = 16, 64
SSM_WIDTH = SSM_HEADS * SSM_HEAD_DIM
SSM_GROUPS, SSM_STATE, SSM_CONV = 2, 128, 4
SSM_CONV_DIM = SSM_WIDTH + 2 * SSM_GROUPS * SSM_STATE
SSM_NORM_EPS = 1e-5
SSM_CHUNK = 128
RWKV_HEADS, RWKV_HEAD_DIM = 8, 64
RWKV_WIDTH = RWKV_HEADS * RWKV_HEAD_DIM
RWKV_DECAY_RANK, RWKV_AAA_RANK, RWKV_MV_RANK, RWKV_GATE_RANK = 96, 96, 64, 256
RWKV_LN_EPS = 64e-5
RWKV_CHUNK = 64
RET_COLS = 4 * RET_WIDTH
SSM_COLS = SSM_WIDTH + SSM_CONV_DIM + SSM_HEADS
RWKV_COLS = 3 * RWKV_WIDTH + RWKV_DECAY_RANK + RWKV_AAA_RANK + RWKV_GATE_RANK
IN_COLS = RET_COLS + SSM_COLS + RWKV_COLS

LANE = 128
VMEM_LIMIT = 56 * 1024 * 1024

C_RET = 0
C_Z = 2048
C_XBC = 3072
C_RKV = 4608
C_XG = 6144
C_DT = 6400
C_XW = 6528
C_XA = 6656
C_PV = 6784
IN_PAD = 6912


def _cparams(sem):
    return pltpu.CompilerParams(dimension_semantics=sem, vmem_limit_bytes=VMEM_LIMIT)


def _split(x, n):
    parts = []
    rem = x
    for i in range(n):
        p = rem.astype(BF16)
        parts.append(p)
        if i + 1 < n:
            rem = rem - p.astype(F32)
    return parts


def _mm(a, b, dims):
    return lax.dot_general(a, b, (dims, ((), ())), preferred_element_type=F32)


def _dot(a, b, pa=1, pb=1, dims=((1,), (0,))):
    ap = _split(a, pa) if a.dtype != BF16 else [a]
    bp = _split(b, pb) if b.dtype != BF16 else [b]
    order = max(len(ap), len(bp))
    acc = None
    for i, x in enumerate(ap):
        for j, y in enumerate(bp):
            if i + j < order:
                t = _mm(x, y, dims)
                acc = t if acc is None else acc + t
    return acc


NT = ((1,), (1,))
TN = ((0,), (0,))


def _sigmoid(x):
    return 1.0 / (1.0 + jnp.exp(-x))


def _silu(x):
    return x * _sigmoid(x)


def _softplus(x):
    return jnp.maximum(x, 0.0) + jnp.log1p(jnp.exp(-jnp.abs(x)))


def _iota(shape, dim):
    return lax.broadcasted_iota(jnp.int32, shape, dim)


def _shift_rows(x, s, tail):
    xs = pltpu.roll(x, s, axis=0)
    tl = pltpu.roll(tail, s, axis=0)
    head = jnp.where(_iota(tl.shape, 0) < s, tl, xs[:8])
    return jnp.concatenate([head, xs[8:]], axis=0)


def _norm_matmul_kernel(x_ref, nw_ref, w_ref, o_ref, h_ref):
    @pl.when(pl.program_id(1) == 0)
    def _():
        x = x_ref[...]
        ms = jnp.mean(x * x, axis=-1, keepdims=True)
        h_ref[...] = (x * lax.rsqrt(ms + NORM_EPS) * nw_ref[...]).astype(BF16)

    o_ref[...] = jnp.dot(h_ref[...], w_ref[...], preferred_element_type=F32)


def _norm_matmul(x, nw, w, tm=1024, tn=768):
    t, d = x.shape
    n = w.shape[1]
    return pl.pallas_call(
        _norm_matmul_kernel,
        out_shape=jax.ShapeDtypeStruct((t, n), F32),
        grid=(t // tm, n // tn),
        in_specs=[pl.BlockSpec((tm, d), lambda i, j: (i, 0)),
                  pl.BlockSpec((1, d), lambda i, j: (0, 0)),
                  pl.BlockSpec((d, tn), lambda i, j: (0, j))],
        out_specs=pl.BlockSpec((tm, tn), lambda i, j: (i, j)),
        scratch_shapes=[pltpu.VMEM((tm, d), BF16)],
        compiler_params=_cparams(("parallel", "arbitrary")),
        name="norm_in_proj",
    )(x, nw.reshape(1, d), w)


def _rope_kernel(cos_ref, sin_ref):
    rows = cos_ref.shape[0]
    half = RET_HEAD_DIM // 2
    pos = (pl.program_id(0) * rows + _iota((rows, LANE), 0)).astype(F32)
    lane = _iota((rows, LANE), 1)
    j = jnp.where(lane >= half, lane - half, lane).astype(F32)
    inv_freq = jnp.exp(j * (-math.log(ROPE_BASE) / half))
    ang = pos * inv_freq
    cos_ref[...] = jnp.cos(ang)
    sin_ref[...] = jnp.where(lane >= half, jnp.sin(ang), -jnp.sin(ang))


def _rope_tables(s, rows=512):
    return pl.pallas_call(
        _rope_kernel,
        out_shape=(jax.ShapeDtypeStruct((s, LANE), F32), jax.ShapeDtypeStruct((s, LANE), F32)),
        grid=(s // rows,),
        out_specs=(pl.BlockSpec((rows, LANE), lambda i: (i, 0)),
                   pl.BlockSpec((rows, LANE), lambda i: (i, 0))),
        compiler_params=_cparams(("parallel",)),
        name="rope_tables",
    )()


def _retention_kernel(q_ref, k_ref, v_ref, g_ref, cos_ref, sin_ref, o_ref, st_ref):
    c, dh = RET_CHUNK, RET_HEAD_DIM

    @pl.when(pl.program_id(1) == 0)
    def _():
        st_ref[...] = jnp.zeros_like(st_ref)

    cos = cos_ref[...]
    sin = sin_ref[...]
    ri = _iota((c, c), 0)
    ci = _iota((c, c), 1)
    rel = (ri - ci).astype(F32)
    causal = ri >= ci
    tcol = _iota((c, 1), 0).astype(F32)
    for h in range(RET_HEADS):
        lg = math.log1p(-(2.0 ** (-5 - h)))
        sl = slice(h * dh, (h + 1) * dh)
        q = q_ref[:, sl]
        k = k_ref[:, sl]
        v = v_ref[:, sl]
        qr = q * cos + pltpu.roll(q, dh // 2, axis=1) * sin
        kr = (k * cos + pltpu.roll(k, dh // 2, axis=1) * sin) * (dh ** -0.5)
        decay = jnp.where(causal, jnp.exp(jnp.maximum(rel, 0.0) * lg), 0.0)
        vb = v.astype(BF16)
        scores = _dot(qr, kr, dims=NT) * decay
        inner = _dot(scores, vb)
        prev = st_ref[h]
        cross = _dot(qr * jnp.exp((tcol + 1.0) * lg), prev)
        kz = kr * jnp.exp((c - 1.0 - tcol) * lg)
        st_ref[h] = prev * math.exp(c * lg) + _dot(kz.T, vb)
        y = inner + cross
        yc = y - jnp.mean(y, axis=-1, keepdims=True)
        yn = yc * lax.rsqrt(jnp.mean(yc * yc, axis=-1, keepdims=True) + RET_GN_EPS)
        o_ref[:, sl] = (yn * _silu(g_ref[:, sl])).astype(BF16)


def _retention(proj, cos, sin, b, s):
    c = RET_CHUNK
    nc = s // c
    w = RET_WIDTH

    def col(j):
        return pl.BlockSpec((c, w), lambda bi, n, j=j: (bi * nc + n, j))

    tab = pl.BlockSpec((c, LANE), lambda bi, n: (n, 0))
    return pl.pallas_call(
        _retention_kernel,
        out_shape=jax.ShapeDtypeStruct((b * s, w), BF16),
        grid=(b, nc),
        in_specs=[col(0), col(1), col(2), col(3), tab, tab],
        out_specs=pl.BlockSpec((c, w), lambda bi, n: (bi * nc + n, 0)),
        scratch_shapes=[pltpu.VMEM((RET_HEADS, RET_HEAD_DIM, RET_HEAD_DIM), F32)],
        compiler_params=_cparams(("parallel", "arbitrary")),
        name="retention",
    )(proj, proj, proj, proj, cos, sin)


def _ssd_kernel(z_ref, xbc_ref, dt_ref, cw_ref, cb_ref, dtb_ref, alog_ref, dsk_ref, nw_ref, ex_ref,
                o_ref, st_ref, tail_ref):
    c, p2, nst = SSM_CHUNK, 2 * SSM_HEAD_DIM, SSM_STATE
    gn = SSM_GROUPS * SSM_STATE

    @pl.when(pl.program_id(1) == 0)
    def _():
        st_ref[...] = jnp.zeros_like(st_ref)
        tail_ref[...] = jnp.zeros_like(tail_ref)

    raw = xbc_ref[...]
    tail = tail_ref[...]
    cw = cw_ref[...]
    conv = raw * cw[3:4] + cb_ref[...]
    for s in range(1, SSM_CONV):
        conv = conv + _shift_rows(raw, s, tail) * cw[3 - s:4 - s]
    tail_ref[...] = raw[c - 8:]
    xbc = _silu(conv)
    xs = xbc[:, :SSM_WIDTH]
    bm = xbc[:, SSM_WIDTH:SSM_WIDTH + gn]
    cm = xbc[:, SSM_WIDTH + gn:]

    dt = _softplus(dt_ref[...] + dtb_ref[...])
    a = dt * (-jnp.exp(alog_ref[...]))
    ri = _iota((c, c), 0)
    ci = _iota((c, c), 1)
    causal = ri >= ci
    tri = jnp.where(causal, 1.0, 0.0).astype(BF16)
    acum = _dot(tri, a, pb=3)
    acum_t = acum.T
    ex = ex_ref[...]
    dt_e = _dot(dt, ex, pa=3)
    acum_e = _dot(acum, ex, pa=3)
    a_last = acum_e[c - 1:c]
    from_start = jnp.exp(acum_e)
    to_end = jnp.exp(a_last - acum_e)
    chunk_decay = jnp.exp(a_last)
    x_dt = xs * dt_e
    lane = _iota((1, p2), 1)
    mask_l = jnp.where(lane < SSM_HEAD_DIM, 1.0, 0.0)
    mask_r = 1.0 - mask_l

    pairs_per_group = SSM_HEADS // SSM_GROUPS // 2
    ys = []
    for g in range(SSM_GROUPS):
        bg = bm[:, g * nst:(g + 1) * nst]
        cg = cm[:, g * nst:(g + 1) * nst]
        cbm = _dot(cg, bg, dims=NT)
        bgt = bg.T.astype(BF16)
        cgb = cg.astype(BF16)
        for pp in range(pairs_per_group):
            p = g * pairs_per_group + pp
            sl = slice(p * p2, (p + 1) * p2)
            xp = x_dt[:, sl]
            y = None
            for hh, msk in ((2 * p, mask_l), (2 * p + 1, mask_r)):
                seg = acum[:, hh:hh + 1] - acum_t[hh:hh + 1, :]
                m = cbm * jnp.where(causal, jnp.exp(jnp.minimum(seg, 0.0)), 0.0)
                t = _dot(m, xp * msk)
                y = t if y is None else y + t
            prev = st_ref[p]
            y = y + _dot(cgb, prev) * from_start[:, sl]
            st_ref[p] = prev * chunk_decay[:, sl] + _dot(bgt, xp * to_end[:, sl])
            ys.append(y + xs[:, sl] * dsk_ref[:, sl])
    y = jnp.concatenate(ys, axis=1) * _silu(z_ref[...])
    gw = SSM_WIDTH // SSM_GROUPS
    nw = nw_ref[...]
    for g in range(SSM_GROUPS):
        sl = slice(g * gw, (g + 1) * gw)
        yg = y[:, sl]
        ms = jnp.mean(yg * yg, axis=-1, keepdims=True)
        o_ref[:, sl] = (yg * lax.rsqrt(ms + SSM_NORM_EPS) * nw[:, sl]).astype(BF16)


def _ssd(proj, conv_w, conv_b, dt_bias, a_log, d_skip, norm_w, b, s):
    c = SSM_CHUNK
    nc = s // c
    pad = LANE - SSM_HEADS
    dtb = jnp.pad(dt_bias, (0, pad)).reshape(1, LANE)
    alog = jnp.pad(a_log, (0, pad)).reshape(1, LANE)
    dsk = jnp.repeat(d_skip, SSM_HEAD_DIM).reshape(1, SSM_WIDTH)
    expand = (jnp.arange(LANE)[:, None] == (jnp.arange(SSM_WIDTH)[None, :] // SSM_HEAD_DIM)).astype(BF16)
    cwp = jnp.pad(conv_w, ((0, 8 - SSM_CONV), (0, 0)))

    def row(wd, j):
        return pl.BlockSpec((c, wd), lambda bi, n, j=j: (bi * nc + n, j))

    def full(r, wd):
        return pl.BlockSpec((r, wd), lambda bi, n: (0, 0))

    return pl.pallas_call(
        _ssd_kernel,
        out_shape=jax.ShapeDtypeStruct((b * s, SSM_WIDTH), BF16),
        grid=(b, nc),
        in_specs=[row(SSM_WIDTH, C_Z // SSM_WIDTH), row(SSM_CONV_DIM, C_XBC // SSM_CONV_DIM),
                  row(LANE, C_DT // LANE),
                  full(8, SSM_CONV_DIM), full(1, SSM_CONV_DIM), full(1, LANE), full(1, LANE),
                  full(1, SSM_WIDTH), full(1, SSM_WIDTH), full(LANE, SSM_WIDTH)],
        out_specs=pl.BlockSpec((c, SSM_WIDTH), lambda bi, n: (bi * nc + n, 0)),
        scratch_shapes=[pltpu.VMEM((SSM_HEADS // 2, SSM_STATE, 2 * SSM_HEAD_DIM), F32),
                        pltpu.VMEM((8, SSM_CONV_DIM), F32)],
        compiler_params=_cparams(("parallel", "arbitrary")),
        name="ssd",
    )(proj, proj, proj, cwp, conv_b.reshape(1, -1), dtb, alog, dsk, norm_w.reshape(1, -1), expand)


RW_P = 2


def _rwkv_kernel(has_vres, *refs):
    if has_vres:
        (r_ref, k_ref, v_ref, xg_ref, xw_ref, xa_ref, pv_ref, vf_ref,
         mu_rkv_ref, mu_g_ref, mu_w_ref, mu_a_ref, mu_v_ref,
         w0_ref, w2_ref, a0_ref, a2_ref, v0_ref, v2_ref, g2_ref, kk_ref, ka_ref, rk_ref,
         lnw_ref, lnb_ref, o_ref, st_ref, tail_ref) = refs
    else:
        (r_ref, k_ref, v_ref, xg_ref, xw_ref, xa_ref,
         mu_rkv_ref, mu_g_ref, mu_w_ref, mu_a_ref,
         w0_ref, w2_ref, a0_ref, a2_ref, g2_ref, kk_ref, ka_ref, rk_ref,
         lnw_ref, lnb_ref, o_ref, vo_ref, st_ref, tail_ref) = refs
    c, w, p2 = RWKV_CHUNK, RWKV_WIDTH, 2 * RWKV_HEAD_DIM

    @pl.when(pl.program_id(1) == 0)
    def _():
        st_ref[...] = jnp.zeros_like(st_ref)
        tail_ref[...] = jnp.zeros_like(tail_ref)

    def mixed(ref, mu, off):
        x = ref[...]
        wd = x.shape[1]
        prev = _shift_rows(x, 1, tail_ref[:, off:off + wd])
        tail_ref[:, off:off + wd] = x[c - 8:]
        return x + (prev - x) * mu

    mu_rkv = mu_rkv_ref[...]
    r = mixed(r_ref, mu_rkv[:, :w], 0)
    k = mixed(k_ref, mu_rkv[:, w:2 * w], w)
    v = mixed(v_ref, mu_rkv[:, 2 * w:], 2 * w)
    xg = mixed(xg_ref, mu_g_ref[...], 3 * w)
    xw = mixed(xw_ref, mu_w_ref[...], 3 * w + 256)
    xa = mixed(xa_ref, mu_a_ref[...], 3 * w + 384)

    w_log = -_softplus(-(w0_ref[...] + _dot(jnp.tanh(xw), w2_ref[...], pa=3, pb=3))) - 0.5
    lw = -jnp.exp(w_log)
    gate = _sigmoid(a0_ref[...] + _dot(xa, a2_ref[...]))
    g = _dot(_sigmoid(xg), g2_ref[...])
    if has_vres:
        pv = mixed(pv_ref, mu_v_ref[...], 3 * w + 512)
        v = v + (vf_ref[...] - v) * _sigmoid(v0_ref[...] + _dot(pv, v2_ref[...]))
    else:
        vo_ref[...] = v

    lane = _iota((1, p2), 1)
    mask_l = jnp.where(lane < RWKV_HEAD_DIM, 1.0, 0.0)
    mask_r = 1.0 - mask_l
    r2 = _iota((p2, p2), 0)
    c2 = _iota((p2, p2), 1)
    same = (r2 < RWKV_HEAD_DIM) == (c2 < RWKV_HEAD_DIM)
    bd_ones = jnp.where(same, 1.0, 0.0).astype(BF16)

    def head_sum(x, pa):
        return jnp.concatenate(
            [_dot(x[:, i * p2:(i + 1) * p2], bd_ones, pa=pa) for i in range(w // p2)], axis=1)

    kk = k * kk_ref[...]
    kk = kk / jnp.maximum(jnp.sqrt(head_sum(kk * kk, 3)), 1e-12)
    k = k * (1.0 + (gate - 1.0) * ka_ref[...])
    kb = kk * gate

    ri = _iota((c, c), 0)
    ci = _iota((c, c), 1)
    tri = jnp.where(ri >= ci, 1.0, 0.0).astype(BF16)
    cum = _dot(tri, lw, pb=3)
    c_last = cum[c - 1:c]
    g_t = jnp.exp(cum)
    g_inv = jnp.exp(-cum)
    g_end = jnp.exp(c_last - cum)
    g_all = jnp.exp(c_last)
    a_s = -kk * jnp.exp(cum - lw)
    r_s = r * g_t
    b_h = kb * g_inv
    k_h = k * g_inv
    b_e = kb * g_end
    k_e = k * g_end

    lower = (r2 >= c2) & same
    strict = (r2 > c2) & same
    eye = jnp.where(r2 == c2, 1.0, 0.0)

    def stack(x):
        return jnp.concatenate([x * mask_l, x * mask_r], axis=0)

    def twice(x):
        return jnp.concatenate([x, x], axis=0)

    dot = functools.partial(_dot, pa=RW_P, pb=RW_P)
    ys = []
    for p in range(w // p2):
        sl = slice(p * p2, (p + 1) * p2)
        sa = stack(a_s[:, sl])
        sr = stack(r_s[:, sl])
        db = twice(b_h[:, sl])
        dk = twice(k_h[:, sl])
        sv = stack(v[:, sl])
        n = jnp.where(strict, dot(sa, db, dims=NT), 0.0)
        a_ak = jnp.where(strict, dot(sa, dk, dims=NT), 0.0)
        a_rb = jnp.where(lower, dot(sr, db, dims=NT), 0.0)
        a_rk = jnp.where(lower, dot(sr, dk, dims=NT), 0.0)
        t = eye + n
        pw = n
        for _ in range(5):
            pw = dot(pw, pw)
            t = t + dot(t, pw)
        av = dot(a_ak, sv)
        a_t = dot(t, sa)
        u_t = dot(t, av)
        r_hat = sr + dot(a_rb, a_t)
        y_in = dot(a_rb, u_t) + dot(a_rk, sv)
        sb = stack(b_e[:, sl]).T
        m = eye * g_all[:, sl] + dot(sb, a_t)
        dh = dot(sb, u_t) + dot(stack(k_e[:, sl]).T, sv)
        h0 = st_ref[p]
        y_st = y_in + dot(r_hat, h0)
        st_ref[p] = dot(m, h0) + dh
        ys.append(y_st[:c] + y_st[c:])
    y = jnp.concatenate(ys, axis=1)

    mean = head_sum(y, 2) * (1.0 / RWKV_HEAD_DIM)
    yc = y - mean
    var = head_sum(yc * yc, 2) * (1.0 / RWKV_HEAD_DIM)
    yn = yc * lax.rsqrt(var + RWKV_LN_EPS) * lnw_ref[...] + lnb_ref[...]
    bonus = head_sum(r * k * rk_ref[...], 2) * v
    o_ref[...] = ((yn + bonus) * g).astype(BF16)


def _rwkv(proj, mu, w0, w2, a0, a2, g2, k_k, k_a, r_k, ln_w, ln_b, vres, b, s):
    c, w = RWKV_CHUNK, RWKV_WIDTH
    nc = s // c
    has_vres = vres is not None
    o1 = 3 * w + RWKV_DECAY_RANK
    o2 = o1 + RWKV_AAA_RANK

    def padc(x, n):
        return jnp.pad(x, (0, n - x.shape[0])).reshape(1, n)

    def padr(x, n):
        return jnp.pad(x, ((0, n - x.shape[0]), (0, 0))).astype(BF16)

    def row(wd, col):
        return pl.BlockSpec((c, wd), lambda bi, n, j=col // wd: (bi * nc + n, j))

    def full(x):
        return pl.BlockSpec(x.shape, lambda bi, n: (0, 0))

    acts = [proj, proj, proj, proj, proj, proj]
    act_specs = [row(w, C_RKV), row(w, C_RKV + w), row(w, C_RKV + 2 * w), row(256, C_XG),
                 row(LANE, C_XW), row(LANE, C_XA)]
    mus = [mu[:3 * w].reshape(1, -1), mu[o2:].reshape(1, -1), padc(mu[3 * w:o1], LANE), padc(mu[o1:o2], LANE)]
    par = [w0.reshape(1, w), padr(w2, LANE), a0.reshape(1, w), padr(a2, LANE)]
    if has_vres:
        mu_v, v0, v2, v_first = vres
        acts += [proj, v_first]
        act_specs += [row(LANE, C_PV), pl.BlockSpec((c, w), lambda bi, n: (bi * nc + n, 0))]
        mus.append(padc(mu_v, LANE))
        par += [v0.reshape(1, w), padr(v2, LANE)]
    par += [g2.astype(BF16), k_k.reshape(1, w), k_a.reshape(1, w), r_k.reshape(1, w),
            ln_w.reshape(1, w), ln_b.reshape(1, w)]
    consts = mus + par
    out_block = pl.BlockSpec((c, w), lambda bi, n: (bi * nc + n, 0))
    if has_vres:
        out_shape = jax.ShapeDtypeStruct((b * s, w), BF16)
        out_specs = out_block
    else:
        out_shape = (jax.ShapeDtypeStruct((b * s, w), BF16), jax.ShapeDtypeStruct((b * s, w), F32))
        out_specs = (out_block, out_block)
    return pl.pallas_call(
        functools.partial(_rwkv_kernel, has_vres),
        out_shape=out_shape,
        grid=(b, nc),
        in_specs=act_specs + [full(x) for x in consts],
        out_specs=out_specs,
        scratch_shapes=[pltpu.VMEM((RWKV_HEADS // 2, 2 * RWKV_HEAD_DIM, 2 * RWKV_HEAD_DIM), F32),
                        pltpu.VMEM((8, 3 * w + 256 + 3 * LANE), F32)],
        compiler_params=_cparams(("parallel", "arbitrary")),
        name="rwkv7",
    )(*acts, *consts)


def _out_proj_kernel(x_ref, yr_ref, ys_ref, yw_ref, wr_ref, ws_ref, ww_ref, o_ref):
    acc = x_ref[...]
    acc = acc + jnp.dot(yr_ref[...], wr_ref[...], preferred_element_type=F32)
    acc = acc + jnp.dot(ys_ref[...], ws_ref[...], preferred_element_type=F32)
    acc = acc + jnp.dot(yw_ref[...], ww_ref[...], preferred_element_type=F32)
    o_ref[...] = acc


def _out_proj(x, y_ret, y_ssm, y_rwkv, w_out, tm=1024, tn=1024):
    t, d = x.shape
    wr = w_out[:RET_WIDTH].astype(BF16)
    ws = w_out[RET_WIDTH:RET_WIDTH + SSM_WIDTH].astype(BF16)
    ww = w_out[RET_WIDTH + SSM_WIDTH:].astype(BF16)

    def rows(wd):
        return pl.BlockSpec((tm, wd), lambda i, j: (i, 0))

    def cols(k):
        return pl.BlockSpec((k, tn), lambda i, j: (0, j))

    return pl.pallas_call(
        _out_proj_kernel,
        out_shape=jax.ShapeDtypeStruct((t, d), F32),
        grid=(t // tm, d // tn),
        in_specs=[pl.BlockSpec((tm, tn), lambda i, j: (i, j)),
                  rows(RET_WIDTH), rows(SSM_WIDTH), rows(RWKV_WIDTH),
                  cols(RET_WIDTH), cols(SSM_WIDTH), cols(RWKV_WIDTH)],
        out_specs=pl.BlockSpec((tm, tn), lambda i, j: (i, j)),
        compiler_params=_cparams(("parallel", "parallel")),
        name="out_proj",
    )(x, y_ret, y_ssm, y_rwkv, wr, ws, ww)


def _ffn_kernel(has_final, *refs):
    if has_final:
        x_ref, nw_ref, wg_ref, wu_ref, wd_ref, fw_ref, o_ref, h_ref, acc_ref = refs
    else:
        x_ref, nw_ref, wg_ref, wu_ref, wd_ref, o_ref, h_ref, acc_ref = refs
    j = pl.program_id(1)

    @pl.when(j == 0)
    def _():
        x = x_ref[...]
        ms = jnp.mean(x * x, axis=-1, keepdims=True)
        h_ref[...] = (x * lax.rsqrt(ms + NORM_EPS) * nw_ref[...]).astype(BF16)
        acc_ref[...] = x

    h = h_ref[...]
    gt = jnp.dot(h, wg_ref[...], preferred_element_type=F32)
    up = jnp.dot(h, wu_ref[...], preferred_element_type=F32)
    act = (_silu(gt) * up).astype(BF16)
    acc_ref[...] += jnp.dot(act, wd_ref[...], preferred_element_type=F32)

    @pl.when(j == pl.num_programs(1) - 1)
    def _():
        y = acc_ref[...]
        if has_final:
            ms = jnp.mean(y * y, axis=-1, keepdims=True)
            y = y * lax.rsqrt(ms + NORM_EPS) * fw_ref[...]
        o_ref[...] = y


def _ffn(x, nw, wg, wu, wd, final_w=None, tm=512, tf=512):
    t, d = x.shape
    f = wg.shape[1]
    has_final = final_w is not None
    in_specs = [pl.BlockSpec((tm, d), lambda i, j: (i, 0)),
                pl.BlockSpec((1, d), lambda i, j: (0, 0)),
                pl.BlockSpec((d, tf), lambda i, j: (0, j)),
                pl.BlockSpec((d, tf), lambda i, j: (0, j)),
                pl.BlockSpec((tf, d), lambda i, j: (j, 0))]
    args = [x, nw.reshape(1, d), wg.astype(BF16), wu.astype(BF16), wd.astype(BF16)]
    if has_final:
        in_specs.append(pl.BlockSpec((1, d), lambda i, j: (0, 0)))
        args.append(final_w.reshape(1, d))
    return pl.pallas_call(
        functools.partial(_ffn_kernel, has_final),
        out_shape=jax.ShapeDtypeStruct((t, d), F32),
        grid=(t // tm, f // tf),
        in_specs=in_specs,
        out_specs=pl.BlockSpec((tm, d), lambda i, j: (i, 0)),
        scratch_shapes=[pltpu.VMEM((tm, d), BF16), pltpu.VMEM((tm, d), F32)],
        compiler_params=_cparams(("parallel", "arbitrary")),
        name="ffn",
    )(*args)


def _pack_w_in(w_in):
    d = w_in.shape[0]
    o_z = RET_COLS
    o_xbc = o_z + SSM_WIDTH
    o_dt = o_xbc + SSM_CONV_DIM
    o_rkv = RET_COLS + SSM_COLS
    o_xw = o_rkv + 3 * RWKV_WIDTH
    o_xa = o_xw + RWKV_DECAY_RANK
    o_xg = o_xa + RWKV_AAA_RANK
    o_pv = IN_COLS

    def seg(lo, hi, width):
        blk = w_in[:, lo:hi]
        return jnp.pad(blk, ((0, 0), (0, width - (hi - lo))))

    pv_hi = w_in.shape[1]
    parts = [seg(0, o_z, RET_COLS), seg(o_z, o_xbc, SSM_WIDTH), seg(o_xbc, o_dt, SSM_CONV_DIM),
             seg(o_rkv, o_xw, 3 * RWKV_WIDTH), seg(o_xg, o_pv, RWKV_GATE_RANK),
             seg(o_dt, o_rkv, LANE), seg(o_xw, o_xa, LANE), seg(o_xa, o_xg, LANE)]
    if pv_hi > o_pv:
        parts.append(seg(o_pv, pv_hi, LANE))
    else:
        parts.append(jnp.zeros((d, LANE), w_in.dtype))
    return jnp.concatenate(parts, axis=1).astype(BF16)


def kernel(x, norm_mix_w, w_in_first, w_in_rest, ssm_conv_w, ssm_conv_b, ssm_dt_bias, ssm_a_log, ssm_d, ssm_norm_w, rwkv_mu, rwkv_mu_v, rwkv_w0, rwkv_w2, rwkv_a0, rwkv_a2, rwkv_v0, rwkv_v2, rwkv_g2, rwkv_k_k, rwkv_k_a, rwkv_r_k, rwkv_ln_w, rwkv_ln_b, w_out, norm_ffn_w, ffn_w_gate, ffn_w_up, ffn_w_down, final_norm_w):
    b, s, d = x.shape
    depth = norm_mix_w.shape[0]
    xf = x.reshape(b * s, d)
    cos, sin = _rope_tables(s)
    v_first = None
    for l in range(depth):
        w_in = w_in_first if l == 0 else w_in_rest[l - 1]
        proj = _norm_matmul(xf, norm_mix_w[l], _pack_w_in(w_in))
        y_ret = _retention(proj, cos, sin, b, s)
        y_ssm = _ssd(proj, ssm_conv_w[l], ssm_conv_b[l], ssm_dt_bias[l], ssm_a_log[l], ssm_d[l],
                     ssm_norm_w[l], b, s)
        vres = None if l == 0 else (rwkv_mu_v[l - 1], rwkv_v0[l - 1], rwkv_v2[l - 1], v_first)
        out = _rwkv(proj, rwkv_mu[l], rwkv_w0[l], rwkv_w2[l], rwkv_a0[l], rwkv_a2[l], rwkv_g2[l],
                    rwkv_k_k[l], rwkv_k_a[l], rwkv_r_k[l], rwkv_ln_w[l], rwkv_ln_b[l], vres, b, s)
        if l == 0:
            y_rwkv, v_first = out
        else:
            y_rwkv = out
        xf = _out_proj(xf, y_ret, y_ssm, y_rwkv, w_out[l])
        xf = _ffn(xf, norm_ffn_w[l], ffn_w_gate[l], ffn_w_up[l], ffn_w_down[l],
                  final_norm_w if l == depth - 1 else None)
    return xf.reshape(b, s, d)
```

```python
import functools
import math

import jax
import jax.numpy as jnp
from jax import lax
from jax.experimental import pallas as pl
from jax.experimental.pallas import tpu as pltpu

F32 = jnp.float32
BF16 = jnp.bfloat16

D_MODEL = 2048
NORM_EPS = 1e-6
RET_HEADS, RET_HEAD_DIM = 4, 128
RET_WIDTH = RET_HEADS * RET_HEAD_DIM
RET_GN_EPS = 1e-6
ROPE_BASE = 10000.0
RET_CHUNK = 128
SSM_HEADS, SSM_HEAD_DIM = 16, 64
SSM_WIDTH = SSM_HEADS * SSM_HEAD_DIM
SSM_GROUPS, SSM_STATE, SSM_CONV = 2, 128, 4
SSM_CONV_DIM = SSM_WIDTH + 2 * SSM_GROUPS * SSM_STATE
SSM_NORM_EPS = 1e-5
SSM_CHUNK = 128
RWKV_HEADS, RWKV_HEAD_DIM = 8, 64
RWKV_WIDTH = RWKV_HEADS * RWKV_HEAD_DIM
RWKV_DECAY_RANK, RWKV_AAA_RANK, RWKV_MV_RANK, RWKV_GATE_RANK = 96, 96, 64, 256
RWKV_LN_EPS = 64e-5
RWKV_CHUNK = 64
RWKV_ROWS = 128
RET_COLS = 4 * RET_WIDTH
SSM_COLS = SSM_WIDTH + SSM_CONV_DIM + SSM_HEADS
RWKV_COLS = 3 * RWKV_WIDTH + RWKV_DECAY_RANK + RWKV_AAA_RANK + RWKV_GATE_RANK
IN_COLS = RET_COLS + SSM_COLS + RWKV_COLS

LANE = 128
VMEM_LIMIT = 56 * 1024 * 1024

C_RET = 0
C_Z = 2048
C_XBC = 3072
C_RKV = 4608
C_XG = 6144
C_DT = 6400
C_XW = 6528
C_XA = 6656
C_PV = 6784
IN_PAD = 6912


def _cparams(sem):
    return pltpu.CompilerParams(dimension_semantics=sem, vmem_limit_bytes=VMEM_LIMIT)


def _split(x, n):
    parts = []
    rem = x
    for i in range(n):
        p = rem.astype(BF16)
        parts.append(p)
        if i + 1 < n:
            rem = rem - p.astype(F32)
    return parts


def _mm(a, b, dims):
    return lax.dot_general(a, b, (dims, ((), ())), preferred_element_type=F32)


def _dot(a, b, pa=1, pb=1, dims=((1,), (0,))):
    ap = _split(a, pa) if a.dtype != BF16 else [a]
    bp = _split(b, pb) if b.dtype != BF16 else [b]
    order = max(len(ap), len(bp))
    acc = None
    for i, x in enumerate(ap):
        for j, y in enumerate(bp):
            if i + j < order:
                t = _mm(x, y, dims)
                acc = t if acc is None else acc + t
    return acc


NT = ((1,), (1,))
TN = ((0,), (0,))


def _sigmoid(x):
    return 1.0 / (1.0 + jnp.exp(-x))


def _silu(x):
    return x * _sigmoid(x)


def _softplus(x):
    return jnp.maximum(x, 0.0) + jnp.log(1.0 + jnp.exp(-jnp.abs(x)))


def _iota(shape, dim):
    return lax.broadcasted_iota(jnp.int32, shape, dim)


def _shift_rows(x, s, tail):
    xs = pltpu.roll(x, s, axis=0)
    tl = pltpu.roll(tail, s, axis=0)
    head = jnp.where(_iota(tl.shape, 0) < s, tl, xs[:8])
    return jnp.concatenate([head, xs[8:]], axis=0)


def _norm_matmul_kernel(x_ref, nw_ref, w_ref, o_ref, h_ref):
    @pl.when(pl.program_id(1) == 0)
    def _():
        x = x_ref[...]
        ms = jnp.mean(x * x, axis=-1, keepdims=True)
        h_ref[...] = (x * lax.rsqrt(ms + NORM_EPS) * nw_ref[...]).astype(BF16)

    o_ref[...] = jnp.dot(h_ref[...], w_ref[...], preferred_element_type=F32)


def _norm_matmul(x, nw, w, tm=1024, tn=768):
    t, d = x.shape
    n = w.shape[1]
    return pl.pallas_call(
        _norm_matmul_kernel,
        out_shape=jax.ShapeDtypeStruct((t, n), F32),
        grid=(t // tm, n // tn),
        in_specs=[pl.BlockSpec((tm, d), lambda i, j: (i, 0)),
                  pl.BlockSpec((1, d), lambda i, j: (0, 0)),
                  pl.BlockSpec((d, tn), lambda i, j: (0, j))],
        out_specs=pl.BlockSpec((tm, tn), lambda i, j: (i, j)),
        scratch_shapes=[pltpu.VMEM((tm, d), BF16)],
        compiler_params=_cparams(("parallel", "arbitrary")),
        name="norm_in_proj",
    )(x, nw.reshape(1, d), w)


def _rope_kernel(cos_ref, sin_ref):
    rows = cos_ref.shape[0]
    half = RET_HEAD_DIM // 2
    pos = (pl.program_id(0) * rows + _iota((rows, LANE), 0)).astype(F32)
    lane = _iota((rows, LANE), 1)
    j = jnp.where(lane >= half, lane - half, lane).astype(F32)
    inv_freq = jnp.exp(j * (-math.log(ROPE_BASE) / half))
    ang = pos * inv_freq
    cos_ref[...] = jnp.cos(ang)
    sin_ref[...] = jnp.where(lane >= half, jnp.sin(ang), -jnp.sin(ang))


def _rope_tables(s, rows=512):
    return pl.pallas_call(
        _rope_kernel,
        out_shape=(jax.ShapeDtypeStruct((s, LANE), F32), jax.ShapeDtypeStruct((s, LANE), F32)),
        grid=(s // rows,),
        out_specs=(pl.BlockSpec((rows, LANE), lambda i: (i, 0)),
                   pl.BlockSpec((rows, LANE), lambda i: (i, 0))),
        compiler_params=_cparams(("parallel",)),
        name="rope_tables",
    )()


def _retention_kernel(q_ref, k_ref, v_ref, g_ref, cos_ref, sin_ref, o_ref, st_ref):
    c, dh = RET_CHUNK, RET_HEAD_DIM

    @pl.when(pl.program_id(1) == 0)
    def _():
        st_ref[...] = jnp.zeros_like(st_ref)

    cos = cos_ref[...]
    sin = sin_ref[...]
    ri = _iota((c, c), 0)
    ci = _iota((c, c), 1)
    rel = (ri - ci).astype(F32)
    causal = ri >= ci
    tcol = _iota((c, 1), 0).astype(F32)
    for h in range(RET_HEADS):
        lg = math.log1p(-(2.0 ** (-5 - h)))
        sl = slice(h * dh, (h + 1) * dh)
        q = q_ref[:, sl]
        k = k_ref[:, sl]
        v = v_ref[:, sl]
        qr = q * cos + pltpu.roll(q, dh // 2, axis=1) * sin
        kr = (k * cos + pltpu.roll(k, dh // 2, axis=1) * sin) * (dh ** -0.5)
        decay = jnp.where(causal, jnp.exp(jnp.maximum(rel, 0.0) * lg), 0.0)
        vb = v.astype(BF16)
        scores = _dot(qr, kr, dims=NT) * decay
        inner = _dot(scores, vb)
        prev = st_ref[h]
        cross = _dot(qr * jnp.exp((tcol + 1.0) * lg), prev)
        kz = kr * jnp.exp((c - 1.0 - tcol) * lg)
        st_ref[h] = prev * math.exp(c * lg) + _dot(kz.T, vb)
        y = inner + cross
        yc = y - jnp.mean(y, axis=-1, keepdims=True)
        yn = yc * lax.rsqrt(jnp.mean(yc * yc, axis=-1, keepdims=True) + RET_GN_EPS)
        o_ref[:, sl] = (yn * _silu(g_ref[:, sl])).astype(BF16)


def _retention(proj, cos, sin, b, s):
    c = RET_CHUNK
    nc = s // c
    w = RET_WIDTH

    def col(j):
        return pl.BlockSpec((c, w), lambda bi, n, j=j: (bi * nc + n, j))

    tab = pl.BlockSpec((c, LANE), lambda bi, n: (n, 0))
    return pl.pallas_call(
        _retention_kernel,
        out_shape=jax.ShapeDtypeStruct((b * s, w), BF16),
        grid=(b, nc),
        in_specs=[col(0), col(1), col(2), col(3), tab, tab],
        out_specs=pl.BlockSpec((c, w), lambda bi, n: (bi * nc + n, 0)),
        scratch_shapes=[pltpu.VMEM((RET_HEADS, RET_HEAD_DIM, RET_HEAD_DIM), F32)],
        compiler_params=_cparams(("parallel", "arbitrary")),
        name="retention",
    )(proj, proj, proj, proj, cos, sin)


def _ssd_kernel(z_ref, xbc_ref, dt_ref, cw_ref, cb_ref, dtb_ref, alog_ref, dsk_ref, nw_ref, ex_ref,
                o_ref, st_ref, tail_ref):
    c, p2, nst = SSM_CHUNK, 2 * SSM_HEAD_DIM, SSM_STATE
    gn = SSM_GROUPS * SSM_STATE

    @pl.when(pl.program_id(1) == 0)
    def _():
        st_ref[...] = jnp.zeros_like(st_ref)
        tail_ref[...] = jnp.zeros_like(tail_ref)

    raw = xbc_ref[...]
    tail = tail_ref[...]
    cw = cw_ref[...]
    conv = raw * cw[3:4] + cb_ref[...]
    for s in range(1, SSM_CONV):
        conv = conv + _shift_rows(raw, s, tail) * cw[3 - s:4 - s]
    tail_ref[...] = raw[c - 8:]
    xbc = _silu(conv)
    xs = xbc[:, :SSM_WIDTH]
    bm = xbc[:, SSM_WIDTH:SSM_WIDTH + gn]
    cm = xbc[:, SSM_WIDTH + gn:]

    dt = _softplus(dt_ref[...] + dtb_ref[...])
    a = dt * (-jnp.exp(alog_ref[...]))
    ri = _iota((c, c), 0)
    ci = _iota((c, c), 1)
    causal = ri >= ci
    tri = jnp.where(causal, 1.0, 0.0).astype(BF16)
    acum = _dot(tri, a, pb=3)
    acum_t = acum.T
    ex = ex_ref[...]
    dt_e = _dot(dt, ex, pa=3)
    acum_e = _dot(acum, ex, pa=3)
    a_last = acum_e[c - 1:c]
    from_start = jnp.exp(acum_e)
    to_end = jnp.exp(a_last - acum_e)
    chunk_decay = jnp.exp(a_last)
    x_dt = xs * dt_e
    lane = _iota((1, p2), 1)
    mask_l = jnp.where(lane < SSM_HEAD_DIM, 1.0, 0.0)
    mask_r = 1.0 - mask_l

    pairs_per_group = SSM_HEADS // SSM_GROUPS // 2
    ys = []
    for g in range(SSM_GROUPS):
        bg = bm[:, g * nst:(g + 1) * nst]
        cg = cm[:, g * nst:(g + 1) * nst]
        cbm = _dot(cg, bg, dims=NT)
        bgt = bg.T.astype(BF16)
        cgb = cg.astype(BF16)
        for pp in range(pairs_per_group):
            p = g * pairs_per_group + pp
            sl = slice(p * p2, (p + 1) * p2)
            xp = x_dt[:, sl]
            y = None
            for hh, msk in ((2 * p, mask_l), (2 * p + 1, mask_r)):
                seg = acum[:, hh:hh + 1] - acum_t[hh:hh + 1, :]
                m = cbm * jnp.where(causal, jnp.exp(jnp.minimum(seg, 0.0)), 0.0)
                t = _dot(m, xp * msk)
                y = t if y is None else y + t
            prev = st_ref[p]
            y = y + _dot(cgb, prev) * from_start[:, sl]
            st_ref[p] = prev * chunk_decay[:, sl] + _dot(bgt, xp * to_end[:, sl])
            ys.append(y + xs[:, sl] * dsk_ref[:, sl])
    y = jnp.concatenate(ys, axis=1) * _silu(z_ref[...])
    gw = SSM_WIDTH // SSM_GROUPS
    nw = nw_ref[...]
    for g in range(SSM_GROUPS):
        sl = slice(g * gw, (g + 1) * gw)
        yg = y[:, sl]
        ms = jnp.mean(yg * yg, axis=-1, keepdims=True)
        o_ref[:, sl] = (yg * lax.rsqrt(ms + SSM_NORM_EPS) * nw[:, sl]).astype(BF16)


def _ssd(proj, conv_w, conv_b, dt_bias, a_log, d_skip, norm_w, b, s):
    c = SSM_CHUNK
    nc = s // c
    pad = LANE - SSM_HEADS
    dtb = jnp.pad(dt_bias, (0, pad)).reshape(1, LANE)
    alog = jnp.pad(a_log, (0, pad)).reshape(1, LANE)
    dsk = jnp.repeat(d_skip, SSM_HEAD_DIM).reshape(1, SSM_WIDTH)
    expand = (jnp.arange(LANE)[:, None] == (jnp.arange(SSM_WIDTH)[None, :] // SSM_HEAD_DIM)).astype(BF16)
    cwp = jnp.pad(conv_w, ((0, 8 - SSM_CONV), (0, 0)))

    def row(wd, j):
        return pl.BlockSpec((c, wd), lambda bi, n, j=j: (bi * nc + n, j))

    def full(r, wd):
        return pl.BlockSpec((r, wd), lambda bi, n: (0, 0))

    return pl.pallas_call(
        _ssd_kernel,
        out_shape=jax.ShapeDtypeStruct((b * s, SSM_WIDTH), BF16),
        grid=(b, nc),
        in_specs=[row(SSM_WIDTH, C_Z // SSM_WIDTH), row(SSM_CONV_DIM, C_XBC // SSM_CONV_DIM),
                  row(LANE, C_DT // LANE),
                  full(8, SSM_CONV_DIM), full(1, SSM_CONV_DIM), full(1, LANE), full(1, LANE),
                  full(1, SSM_WIDTH), full(1, SSM_WIDTH), full(LANE, SSM_WIDTH)],
        out_specs=pl.BlockSpec((c, SSM_WIDTH), lambda bi, n: (bi * nc + n, 0)),
        scratch_shapes=[pltpu.VMEM((SSM_HEADS // 2, SSM_STATE, 2 * SSM_HEAD_DIM), F32),
                        pltpu.VMEM((8, SSM_CONV_DIM), F32)],
        compiler_params=_cparams(("parallel", "arbitrary")),
        name="ssd",
    )(proj, proj, proj, cwp, conv_b.reshape(1, -1), dtb, alog, dsk, norm_w.reshape(1, -1), expand)


def _rwkv_kernel(has_vres, *refs):
    if has_vres:
        (r_ref, k_ref, v_ref, xg_ref, xw_ref, xa_ref, pv_ref, vf_ref,
         mu_rkv_ref, mu_g_ref, mu_w_ref, mu_a_ref, mu_v_ref,
         w0_ref, w2_ref, a0_ref, a2_ref, v0_ref, v2_ref, g2_ref, kk_ref, ka_ref, rk_ref,
         lnw_ref, lnb_ref, o_ref, st_ref, tail_ref) = refs
    else:
        (r_ref, k_ref, v_ref, xg_ref, xw_ref, xa_ref,
         mu_rkv_ref, mu_g_ref, mu_w_ref, mu_a_ref,
         w0_ref, w2_ref, a0_ref, a2_ref, g2_ref, kk_ref, ka_ref, rk_ref,
         lnw_ref, lnb_ref, o_ref, vo_ref, st_ref, tail_ref) = refs
    c, w, p2 = RWKV_CHUNK, RWKV_WIDTH, 2 * RWKV_HEAD_DIM
    rows = o_ref.shape[0]
    nsub = rows // c

    @pl.when(pl.program_id(1) == 0)
    def _():
        st_ref[...] = jnp.zeros_like(st_ref)
        tail_ref[...] = jnp.zeros_like(tail_ref)

    def mixed(ref, mu, off):
        x = ref[...]
        wd = x.shape[1]
        prev = _shift_rows(x, 1, tail_ref[:, off:off + wd])
        tail_ref[:, off:off + wd] = x[rows - 8:]
        return x + (prev - x) * mu

    mu_rkv = mu_rkv_ref[...]
    r = mixed(r_ref, mu_rkv[:, :w], 0)
    k = mixed(k_ref, mu_rkv[:, w:2 * w], w)
    v = mixed(v_ref, mu_rkv[:, 2 * w:], 2 * w)
    xg = mixed(xg_ref, mu_g_ref[...], 3 * w)
    xw = mixed(xw_ref, mu_w_ref[...], 3 * w + 256)
    xa = mixed(xa_ref, mu_a_ref[...], 3 * w + 384)

    w_log = -_softplus(-(w0_ref[...] + _dot(jnp.tanh(xw), w2_ref[...], pa=3, pb=3))) - 0.5
    lw = -jnp.exp(w_log)
    gate = _sigmoid(a0_ref[...] + _dot(xa, a2_ref[...]))
    g = _dot(_sigmoid(xg), g2_ref[...])
    if has_vres:
        pv = mixed(pv_ref, mu_v_ref[...], 3 * w + 512)
        v = v + (vf_ref[...] - v) * _sigmoid(v0_ref[...] + _dot(pv, v2_ref[...]))
    else:
        vo_ref[...] = v

    lane = _iota((1, p2), 1)
    mask_l = jnp.where(lane < RWKV_HEAD_DIM, 1.0, 0.0)
    mask_r = 1.0 - mask_l
    r2 = _iota((p2, p2), 0)
    c2 = _iota((p2, p2), 1)
    same = (r2 < RWKV_HEAD_DIM) == (c2 < RWKV_HEAD_DIM)
    bd_ones = jnp.where(same, 1.0, 0.0).astype(BF16)

    def head_sum(x, pa):
        return jnp.concatenate(
            [_dot(x[:, i * p2:(i + 1) * p2], bd_ones, pa=pa) for i in range(w // p2)], axis=1)

    kk = k * kk_ref[...]
    kk = kk * jnp.minimum(lax.rsqrt(head_sum(kk * kk, 1)), 1e12)
    k = k * (1.0 + (gate - 1.0) * ka_ref[...])
    kb = kk * gate

    ri = _iota((rows, rows), 0)
    ci = _iota((rows, rows), 1)
    in_chunk = (ri >= ci) & ((ri - ci) <= (ri & (c - 1)))
    tri = jnp.where(in_chunk, 1.0, 0.0).astype(BF16)
    cum = _dot(tri, lw, pb=2)
    c_ends = [cum[(j + 1) * c - 1:(j + 1) * c] for j in range(nsub)]
    c_last = jnp.concatenate([jnp.broadcast_to(x, (c, w)) for x in c_ends], axis=0)
    g_t = jnp.exp(cum)
    g_inv = jnp.exp(-cum)
    g_end = jnp.exp(c_last - cum)
    g_all = [jnp.exp(x) for x in c_ends]
    a_s = -kk * jnp.exp(cum - lw)
    r_s = r * g_t
    b_h = kb * g_inv
    k_h = k * g_inv
    b_e = kb * g_end
    k_e = k * g_end

    lower = (r2 >= c2) & same
    strict = (r2 > c2) & same
    eye = jnp.where(r2 == c2, 1.0, 0.0)

    def stack(x):
        return jnp.concatenate([x * mask_l, x * mask_r], axis=0)

    def twice(x):
        return jnp.concatenate([x, x], axis=0)

    def bf(x):
        return x.astype(BF16)

    def mm(a, b):
        return _mm(a, b, ((1,), (0,)))

    npair = w // p2
    pairs = range(nsub * npair)
    sls = [(slice((q // npair) * c, (q // npair + 1) * c), slice((q % npair) * p2, (q % npair + 1) * p2))
           for q in pairs]
    sa = [bf(stack(a_s[s])) for s in sls]
    sr = [stack(r_s[s]) for s in sls]
    sv = [bf(stack(v[s])) for s in sls]
    gram = [_mm(jnp.concatenate([sa[p], bf(sr[p])], axis=0),
                bf(jnp.concatenate([twice(b_h[sls[p]]), twice(k_h[sls[p]])], axis=0)), NT)
            for p in pairs]
    n = [jnp.where(strict, x[:p2, :p2], 0.0) for x in gram]
    a_ak = [bf(jnp.where(strict, x[:p2, p2:], 0.0)) for x in gram]
    a_rb = [bf(jnp.where(lower, x[p2:, :p2], 0.0)) for x in gram]
    a_rk = [bf(jnp.where(lower, x[p2:, p2:], 0.0)) for x in gram]
    t = [eye + x for x in n]
    pw = [bf(x) for x in n]
    for _ in range(5):
        pw = [bf(mm(x, x)) for x in pw]
        t = [t[p] + mm(bf(t[p]), pw[p]) for p in pairs]
    av = [bf(mm(a_ak[p], sv[p])) for p in pairs]
    au = [bf(mm(bf(t[p]), jnp.concatenate([sa[p], av[p]], axis=1))) for p in pairs]
    zero = jnp.zeros((p2, p2), BF16)
    z = [mm(jnp.concatenate([jnp.concatenate([a_rb[p], a_rk[p]], axis=1),
                             jnp.concatenate([bf(stack(b_e[sls[p]]).T), bf(stack(k_e[sls[p]]).T)], axis=1)],
                            axis=0),
            jnp.concatenate([au[p], jnp.concatenate([zero, sv[p]], axis=1)], axis=0))
         for p in pairs]
    lhs = [bf(jnp.concatenate([sr[p] + z[p][:p2, :p2], z[p][p2:, :p2]], axis=0)) for p in pairs]
    g_col = [jnp.sum(eye * g_all[q // npair][:, sls[q][1]], axis=1, keepdims=True) for q in pairs]
    ys = [[None] * npair for _ in range(nsub)]
    for p in range(npair):
        h = st_ref[p]
        for j in range(nsub):
            q = j * npair + p
            hi, lo = _split(h, 2)
            out = mm(jnp.concatenate([lhs[q], lhs[q]], axis=1), jnp.concatenate([hi, lo], axis=0)) + z[q][:, p2:]
            h = out[p2:] + g_col[q] * h
            ys[j][p] = out[:c] + out[c:p2]
        st_ref[p] = h
    y = jnp.concatenate([jnp.concatenate(row, axis=1) for row in ys], axis=0)

    mean = head_sum(y, 1) * (1.0 / RWKV_HEAD_DIM)
    yc = y - mean
    var = head_sum(yc * yc, 1) * (1.0 / RWKV_HEAD_DIM)
    yn = yc * lax.rsqrt(var + RWKV_LN_EPS) * lnw_ref[...] + lnb_ref[...]
    bonus = head_sum(r * k * rk_ref[...], 1) * v
    o_ref[...] = ((yn + bonus) * g).astype(BF16)


def _rwkv(proj, mu, w0, w2, a0, a2, g2, k_k, k_a, r_k, ln_w, ln_b, vres, b, s):
    c, w = RWKV_ROWS, RWKV_WIDTH
    nc = s // c
    has_vres = vres is not None
    o1 = 3 * w + RWKV_DECAY_RANK
    o2 = o1 + RWKV_AAA_RANK

    def padc(x, n):
        return jnp.pad(x, (0, n - x.shape[0])).reshape(1, n)

    def padr(x, n):
        return jnp.pad(x, ((0, n - x.shape[0]), (0, 0))).astype(BF16)

    def row(wd, col):
        return pl.BlockSpec((c, wd), lambda bi, n, j=col // wd: (bi * nc + n, j))

    def full(x):
        return pl.BlockSpec(x.shape, lambda bi, n: (0, 0))

    acts = [proj, proj, proj, proj, proj, proj]
    act_specs = [row(w, C_RKV), row(w, C_RKV + w), row(w, C_RKV + 2 * w), row(256, C_XG),
                 row(LANE, C_XW), row(LANE, C_XA)]
    mus = [mu[:3 * w].reshape(1, -1), mu[o2:].reshape(1, -1), padc(mu[3 * w:o1], LANE), padc(mu[o1:o2], LANE)]
    par = [w0.reshape(1, w), padr(w2, LANE), a0.reshape(1, w), padr(a2, LANE)]
    if has_vres:
        mu_v, v0, v2, v_first = vres
        acts += [proj, v_first]
        act_specs += [row(LANE, C_PV), pl.BlockSpec((c, w), lambda bi, n: (bi * nc + n, 0))]
        mus.append(padc(mu_v, LANE))
        par += [v0.reshape(1, w), padr(v2, LANE)]
    par += [g2.astype(BF16), k_k.reshape(1, w), k_a.reshape(1, w), r_k.reshape(1, w),
            ln_w.reshape(1, w), ln_b.reshape(1, w)]
    consts = mus + par
    out_block = pl.BlockSpec((c, w), lambda bi, n: (bi * nc + n, 0))
    if has_vres:
        out_shape = jax.ShapeDtypeStruct((b * s, w), BF16)
        out_specs = out_block
    else:
        out_shape = (jax.ShapeDtypeStruct((b * s, w), BF16), jax.ShapeDtypeStruct((b * s, w), F32))
        out_specs = (out_block, out_block)
    return pl.pallas_call(
        functools.partial(_rwkv_kernel, has_vres),
        out_shape=out_shape,
        grid=(b, nc),
        in_specs=act_specs + [full(x) for x in consts],
        out_specs=out_specs,
        scratch_shapes=[pltpu.VMEM((RWKV_HEADS // 2, 2 * RWKV_HEAD_DIM, 2 * RWKV_HEAD_DIM), F32),
                        pltpu.VMEM((8, 3 * w + 256 + 3 * LANE), F32)],
        compiler_params=_cparams(("parallel", "arbitrary")),
        name="rwkv7",
    )(*acts, *consts)


def _out_proj_kernel(x_ref, yr_ref, ys_ref, yw_ref, wr_ref, ws_ref, ww_ref, o_ref):
    acc = x_ref[...]
    acc = acc + jnp.dot(yr_ref[...], wr_ref[...], preferred_element_type=F32)
    acc = acc + jnp.dot(ys_ref[...], ws_ref[...], preferred_element_type=F32)
    acc = acc + jnp.dot(yw_ref[...], ww_ref[...], preferred_element_type=F32)
    o_ref[...] = acc


def _out_proj(x, y_ret, y_ssm, y_rwkv, w_out, tm=1024, tn=1024):
    t, d = x.shape
    wr = w_out[:RET_WIDTH].astype(BF16)
    ws = w_out[RET_WIDTH:RET_WIDTH + SSM_WIDTH].astype(BF16)
    ww = w_out[RET_WIDTH + SSM_WIDTH:].astype(BF16)

    def rows(wd):
        return pl.BlockSpec((tm, wd), lambda i, j: (i, 0))

    def cols(k):
        return pl.BlockSpec((k, tn), lambda i, j: (0, j))

    return pl.pallas_call(
        _out_proj_kernel,
        out_shape=jax.ShapeDtypeStruct((t, d), F32),
        grid=(t // tm, d // tn),
        in_specs=[pl.BlockSpec((tm, tn), lambda i, j: (i, j)),
                  rows(RET_WIDTH), rows(SSM_WIDTH), rows(RWKV_WIDTH),
                  cols(RET_WIDTH), cols(SSM_WIDTH), cols(RWKV_WIDTH)],
        out_specs=pl.BlockSpec((tm, tn), lambda i, j: (i, j)),
        compiler_params=_cparams(("parallel", "parallel")),
        name="out_proj",
    )(x, y_ret, y_ssm, y_rwkv, wr, ws, ww)


def _ffn_kernel(has_final, *refs):
    if has_final:
        x_ref, nw_ref, wg_ref, wu_ref, wd_ref, fw_ref, o_ref, h_ref = refs
    else:
        x_ref, nw_ref, wg_ref, wu_ref, wd_ref, o_ref, h_ref = refs
    j = pl.program_id(1)

    @pl.when(j == 0)
    def _():
        x = x_ref[...]
        ms = jnp.mean(x * x, axis=-1, keepdims=True)
        h_ref[...] = (x * lax.rsqrt(ms + NORM_EPS) * nw_ref[...]).astype(BF16)
        o_ref[...] = x

    h = h_ref[...]
    gt = jnp.dot(h, wg_ref[...], preferred_element_type=F32)
    up = jnp.dot(h, wu_ref[...], preferred_element_type=F32)
    act = (_silu(gt) * up).astype(BF16)
    o_ref[...] += jnp.dot(act, wd_ref[...], preferred_element_type=F32)

    if has_final:
        @pl.when(j == pl.num_programs(1) - 1)
        def _():
            y = o_ref[...]
            ms = jnp.mean(y * y, axis=-1, keepdims=True)
            o_ref[...] = y * lax.rsqrt(ms + NORM_EPS) * fw_ref[...]


def _ffn(x, nw, wg, wu, wd, final_w=None, tm=1024, tf=512):
    t, d = x.shape
    f = wg.shape[1]
    has_final = final_w is not None
    in_specs = [pl.BlockSpec((tm, d), lambda i, j: (i, 0)),
                pl.BlockSpec((1, d), lambda i, j: (0, 0)),
                pl.BlockSpec((d, tf), lambda i, j: (0, j)),
                pl.BlockSpec((d, tf), lambda i, j: (0, j)),
                pl.BlockSpec((tf, d), lambda i, j: (j, 0))]
    args = [x, nw.reshape(1, d), wg.astype(BF16), wu.astype(BF16), wd.astype(BF16)]
    if has_final:
        in_specs.append(pl.BlockSpec((1, d), lambda i, j: (0, 0)))
        args.append(final_w.reshape(1, d))
    return pl.pallas_call(
        functools.partial(_ffn_kernel, has_final),
        out_shape=jax.ShapeDtypeStruct((t, d), F32),
        grid=(t // tm, f // tf),
        in_specs=in_specs,
        out_specs=pl.BlockSpec((tm, d), lambda i, j: (i, 0)),
        scratch_shapes=[pltpu.VMEM((tm, d), BF16)],
        compiler_params=_cparams(("parallel", "arbitrary")),
        name="ffn",
    )(*args)


def _pack_w_in(w_in):
    d = w_in.shape[0]
    o_z = RET_COLS
    o_xbc = o_z + SSM_WIDTH
    o_dt = o_xbc + SSM_CONV_DIM
    o_rkv = RET_COLS + SSM_COLS
    o_xw = o_rkv + 3 * RWKV_WIDTH
    o_xa = o_xw + RWKV_DECAY_RANK
    o_xg = o_xa + RWKV_AAA_RANK
    o_pv = IN_COLS

    def seg(lo, hi, width):
        blk = w_in[:, lo:hi]
        return jnp.pad(blk, ((0, 0), (0, width - (hi - lo))))

    pv_hi = w_in.shape[1]
    parts = [seg(0, o_z, RET_COLS), seg(o_z, o_xbc, SSM_WIDTH), seg(o_xbc, o_dt, SSM_CONV_DIM),
             seg(o_rkv, o_xw, 3 * RWKV_WIDTH), seg(o_xg, o_pv, RWKV_GATE_RANK),
             seg(o_dt, o_rkv, LANE), seg(o_xw, o_xa, LANE), seg(o_xa, o_xg, LANE)]
    if pv_hi > o_pv:
        parts.append(seg(o_pv, pv_hi, LANE))
    else:
        parts.append(jnp.zeros((d, LANE), w_in.dtype))
    return jnp.concatenate(parts, axis=1).astype(BF16)


def kernel(x, norm_mix_w, w_in_first, w_in_rest, ssm_conv_w, ssm_conv_b, ssm_dt_bias, ssm_a_log, ssm_d, ssm_norm_w, rwkv_mu, rwkv_mu_v, rwkv_w0, rwkv_w2, rwkv_a0, rwkv_a2, rwkv_v0, rwkv_v2, rwkv_g2, rwkv_k_k, rwkv_k_a, rwkv_r_k, rwkv_ln_w, rwkv_ln_b, w_out, norm_ffn_w, ffn_w_gate, ffn_w_up, ffn_w_down, final_norm_w):
    b, s, d = x.shape
    depth = norm_mix_w.shape[0]
    xf = x.reshape(b * s, d)
    cos, sin = _rope_tables(s)
    v_first = None
    for l in range(depth):
        w_in = w_in_first if l == 0 else w_in_rest[l - 1]
        proj = _norm_matmul(xf, norm_mix_w[l], _pack_w_in(w_in))
        y_ret = _retention(proj, cos, sin, b, s)
        y_ssm = _ssd(proj, ssm_conv_w[l], ssm_conv_b[l], ssm_dt_bias[l], ssm_a_log[l], ssm_d[l],
                     ssm_norm_w[l], b, s)
        vres = None if l == 0 else (rwkv_mu_v[l - 1], rwkv_v0[l - 1], rwkv_v2[l - 1], v_first)
        out = _rwkv(proj, rwkv_mu[l], rwkv_w0[l], rwkv_w2[l], rwkv_a0[l], rwkv_a2[l], rwkv_g2[l],
                    rwkv_k_k[l], rwkv_k_a[l], rwkv_r_k[l], rwkv_ln_w[l], rwkv_ln_b[l], vres, b, s)
        if l == 0:
            y_rwkv, v_first = out
        else:
            y_rwkv = out
        xf = _out_proj(xf, y_ret, y_ssm, y_rwkv, w_out[l])
        xf = _ffn(xf, norm_ffn_w[l], ffn_w_gate[l], ffn_w_up[l], ffn_w_down[l],
                  final_norm_w if l == depth - 1 else None)
    return xf.reshape(b, s, d)
```

```python
import functools
import math

import jax
import jax.numpy as jnp
from jax import lax
from jax.experimental import pallas as pl
from jax.experimental.pallas import tpu as pltpu

F32 = jnp.float32
BF16 = jnp.bfloat16

D_MODEL = 2048
NORM_EPS = 1e-6
RET_HEADS, RET_HEAD_DIM = 4, 128
RET_WIDTH = RET_HEADS * RET_HEAD_DIM
RET_GN_EPS = 1e-6
ROPE_BASE = 10000.0
RET_CHUNK = 128
SSM_HEADS, SSM_HEAD_DIM = 16, 64
SSM_WIDTH = SSM_HEADS * SSM_HEAD_DIM
SSM_GROUPS, SSM_STATE, SSM_CONV = 2, 128, 4
SSM_CONV_DIM = SSM_WIDTH + 2 * SSM_GROUPS * SSM_STATE
SSM_NORM_EPS = 1e-5
SSM_CHUNK = 128
RWKV_HEADS, RWKV_HEAD_DIM = 8, 64
RWKV_WIDTH = RWKV_HEADS * RWKV_HEAD_DIM
RWKV_DECAY_RANK, RWKV_AAA_RANK, RWKV_MV_RANK, RWKV_GATE_RANK = 96, 96, 64, 256
RWKV_LN_EPS = 64e-5
RWKV_CHUNK = 64
RWKV_ROWS = 128
RET_COLS = 4 * RET_WIDTH
SSM_COLS = SSM_WIDTH + SSM_CONV_DIM + SSM_HEADS
RWKV_COLS = 3 * RWKV_WIDTH + RWKV_DECAY_RANK + RWKV_AAA_RANK + RWKV_GATE_RANK
IN_COLS = RET_COLS + SSM_COLS + RWKV_COLS

LANE = 128
VMEM_LIMIT = 56 * 1024 * 1024

C_RET = 0
C_Z = 2048
C_XBC = 3072
C_RKV = 4608
C_XG = 6144
C_DT = 6400
C_XW = 6528
C_XA = 6656
C_PV = 6784
IN_PAD = 6912


def _cparams(sem):
    return pltpu.CompilerParams(dimension_semantics=sem, vmem_limit_bytes=VMEM_LIMIT)


def _split(x, n):
    parts = []
    rem = x
    for i in range(n):
        p = rem.astype(BF16)
        parts.append(p)
        if i + 1 < n:
            rem = rem - p.astype(F32)
    return parts


def _mm(a, b, dims):
    return lax.dot_general(a, b, (dims, ((), ())), preferred_element_type=F32)


def _dot(a, b, pa=1, pb=1, dims=((1,), (0,))):
    ap = _split(a, pa) if a.dtype != BF16 else [a]
    bp = _split(b, pb) if b.dtype != BF16 else [b]
    order = max(len(ap), len(bp))
    acc = None
    for i, x in enumerate(ap):
        for j, y in enumerate(bp):
            if i + j < order:
                t = _mm(x, y, dims)
                acc = t if acc is None else acc + t
    return acc


NT = ((1,), (1,))
TN = ((0,), (0,))


def _sigmoid(x):
    return 1.0 / (1.0 + jnp.exp(-x))


def _silu(x):
    return x * _sigmoid(x)


def _softplus(x):
    return jnp.maximum(x, 0.0) + jnp.log(1.0 + jnp.exp(-jnp.abs(x)))


def _iota(shape, dim):
    return lax.broadcasted_iota(jnp.int32, shape, dim)


def _shift_rows(x, s, tail):
    xs = pltpu.roll(x, s, axis=0)
    tl = pltpu.roll(tail, s, axis=0)
    head = jnp.where(_iota(tl.shape, 0) < s, tl, xs[:8])
    return jnp.concatenate([head, xs[8:]], axis=0)


def _norm_matmul_kernel(x_ref, nw_ref, w_ref, o_ref, h_ref):
    @pl.when(pl.program_id(1) == 0)
    def _():
        x = x_ref[...]
        ms = jnp.mean(x * x, axis=-1, keepdims=True)
        h_ref[...] = (x * lax.rsqrt(ms + NORM_EPS) * nw_ref[...]).astype(BF16)

    o_ref[...] = jnp.dot(h_ref[...], w_ref[...], preferred_element_type=F32)


def _norm_matmul(x, nw, w, tm=1024, tn=768):
    t, d = x.shape
    n = w.shape[1]
    return pl.pallas_call(
        _norm_matmul_kernel,
        out_shape=jax.ShapeDtypeStruct((t, n), F32),
        grid=(t // tm, n // tn),
        in_specs=[pl.BlockSpec((tm, d), lambda i, j: (i, 0)),
                  pl.BlockSpec((1, d), lambda i, j: (0, 0)),
                  pl.BlockSpec((d, tn), lambda i, j: (0, j))],
        out_specs=pl.BlockSpec((tm, tn), lambda i, j: (i, j)),
        scratch_shapes=[pltpu.VMEM((tm, d), BF16)],
        compiler_params=_cparams(("parallel", "arbitrary")),
        name="norm_in_proj",
    )(x, nw.reshape(1, d), w)


def _rope_kernel(cos_ref, sin_ref):
    rows = cos_ref.shape[0]
    half = RET_HEAD_DIM // 2
    pos = (pl.program_id(0) * rows + _iota((rows, LANE), 0)).astype(F32)
    lane = _iota((rows, LANE), 1)
    j = jnp.where(lane >= half, lane - half, lane).astype(F32)
    inv_freq = jnp.exp(j * (-math.log(ROPE_BASE) / half))
    ang = pos * inv_freq
    cos_ref[...] = jnp.cos(ang)
    sin_ref[...] = jnp.where(lane >= half, jnp.sin(ang), -jnp.sin(ang))


def _rope_tables(s, rows=512):
    return pl.pallas_call(
        _rope_kernel,
        out_shape=(jax.ShapeDtypeStruct((s, LANE), F32), jax.ShapeDtypeStruct((s, LANE), F32)),
        grid=(s // rows,),
        out_specs=(pl.BlockSpec((rows, LANE), lambda i: (i, 0)),
                   pl.BlockSpec((rows, LANE), lambda i: (i, 0))),
        compiler_params=_cparams(("parallel",)),
        name="rope_tables",
    )()


def _retention_body(q_ref, k_ref, v_ref, g_ref, cos_ref, sin_ref, o_ref, st_ref):
    c, dh = RET_CHUNK, RET_HEAD_DIM

    @pl.when(pl.program_id(1) == 0)
    def _():
        st_ref[...] = jnp.zeros_like(st_ref)

    yield
    cos = cos_ref[...]
    sin = sin_ref[...]
    ri = _iota((c, c), 0)
    ci = _iota((c, c), 1)
    rel = (ri - ci).astype(F32)
    causal = ri >= ci
    tcol = _iota((c, 1), 0).astype(F32)
    for h in range(RET_HEADS):
        lg = math.log1p(-(2.0 ** (-5 - h)))
        sl = slice(h * dh, (h + 1) * dh)
        q = q_ref[:, sl]
        k = k_ref[:, sl]
        v = v_ref[:, sl]
        qr = q * cos + pltpu.roll(q, dh // 2, axis=1) * sin
        kr = (k * cos + pltpu.roll(k, dh // 2, axis=1) * sin) * (dh ** -0.5)
        decay = jnp.where(causal, jnp.exp(jnp.maximum(rel, 0.0) * lg), 0.0)
        vb = v.astype(BF16)
        scores = _dot(qr, kr, dims=NT) * decay
        yield
        inner = _dot(scores, vb)
        prev = st_ref[h]
        cross = _dot(qr * jnp.exp((tcol + 1.0) * lg), prev)
        kz = kr * jnp.exp((c - 1.0 - tcol) * lg)
        st_ref[h] = prev * math.exp(c * lg) + _dot(kz.T, vb)
        yield
        y = inner + cross
        yc = y - jnp.mean(y, axis=-1, keepdims=True)
        yn = yc * lax.rsqrt(jnp.mean(yc * yc, axis=-1, keepdims=True) + RET_GN_EPS)
        o_ref[:, sl] = (yn * _silu(g_ref[:, sl])).astype(BF16)
        yield


def _ssd_body(z_ref, xbc_ref, dt_ref, cw_ref, cb_ref, dtb_ref, alog_ref, dsk_ref, nw_ref, ex_ref,
              o_ref, st_ref, tail_ref):
    c, p2, nst = SSM_CHUNK, 2 * SSM_HEAD_DIM, SSM_STATE
    gn = SSM_GROUPS * SSM_STATE

    @pl.when(pl.program_id(1) == 0)
    def _():
        st_ref[...] = jnp.zeros_like(st_ref)
        tail_ref[...] = jnp.zeros_like(tail_ref)

    yield
    raw = xbc_ref[...]
    tail = tail_ref[...]
    cw = cw_ref[...]
    conv = raw * cw[3:4] + cb_ref[...]
    for s in range(1, SSM_CONV):
        conv = conv + _shift_rows(raw, s, tail) * cw[3 - s:4 - s]
    tail_ref[...] = raw[c - 8:]
    yield
    xbc = _silu(conv)
    xs = xbc[:, :SSM_WIDTH]
    bm = xbc[:, SSM_WIDTH:SSM_WIDTH + gn]
    cm = xbc[:, SSM_WIDTH + gn:]
    yield

    dt = _softplus(dt_ref[...] + dtb_ref[...])
    a = dt * (-jnp.exp(alog_ref[...]))
    ri = _iota((c, c), 0)
    ci = _iota((c, c), 1)
    causal = ri >= ci
    tri = jnp.where(causal, 1.0, 0.0).astype(BF16)
    acum = _dot(tri, a, pb=3)
    acum_t = acum.T
    ex = ex_ref[...]
    dt_e = _dot(dt, ex, pa=3)
    acum_e = _dot(acum, ex, pa=3)
    a_last = acum_e[c - 1:c]
    from_start = jnp.exp(acum_e)
    to_end = jnp.exp(a_last - acum_e)
    chunk_decay = jnp.exp(a_last)
    x_dt = xs * dt_e
    lane = _iota((1, p2), 1)
    mask_l = jnp.where(lane < SSM_HEAD_DIM, 1.0, 0.0)
    mask_r = 1.0 - mask_l
    yield

    pairs_per_group = SSM_HEADS // SSM_GROUPS // 2
    ys = []
    for g in range(SSM_GROUPS):
        bg = bm[:, g * nst:(g + 1) * nst]
        cg = cm[:, g * nst:(g + 1) * nst]
        cbm = _dot(cg, bg, dims=NT)
        bgt = bg.T.astype(BF16)
        cgb = cg.astype(BF16)
        for pp in range(pairs_per_group):
            p = g * pairs_per_group + pp
            sl = slice(p * p2, (p + 1) * p2)
            xp = x_dt[:, sl]
            y = None
            for hh, msk in ((2 * p, mask_l), (2 * p + 1, mask_r)):
                seg = acum[:, hh:hh + 1] - acum_t[hh:hh + 1, :]
                m = cbm * jnp.where(causal, jnp.exp(jnp.minimum(seg, 0.0)), 0.0)
                t = _dot(m, xp * msk)
                y = t if y is None else y + t
            prev = st_ref[p]
            y = y + _dot(cgb, prev) * from_start[:, sl]
            st_ref[p] = prev * chunk_decay[:, sl] + _dot(bgt, xp * to_end[:, sl])
            ys.append(y + xs[:, sl] * dsk_ref[:, sl])
            yield
    y = jnp.concatenate(ys, axis=1) * _silu(z_ref[...])
    gw = SSM_WIDTH // SSM_GROUPS
    nw = nw_ref[...]
    for g in range(SSM_GROUPS):
        sl = slice(g * gw, (g + 1) * gw)
        yg = y[:, sl]
        ms = jnp.mean(yg * yg, axis=-1, keepdims=True)
        o_ref[:, sl] = (yg * lax.rsqrt(ms + SSM_NORM_EPS) * nw[:, sl]).astype(BF16)
        yield


def _rwkv_body(has_vres, *refs):
    if has_vres:
        (r_ref, k_ref, v_ref, xg_ref, xw_ref, xa_ref, pv_ref, vf_ref,
         mu_rkv_ref, mu_g_ref, mu_w_ref, mu_a_ref, mu_v_ref,
         w0_ref, w2_ref, a0_ref, a2_ref, v0_ref, v2_ref, g2_ref, kk_ref, ka_ref, rk_ref,
         lnw_ref, lnb_ref, o_ref, st_ref, tail_ref) = refs
    else:
        (r_ref, k_ref, v_ref, xg_ref, xw_ref, xa_ref,
         mu_rkv_ref, mu_g_ref, mu_w_ref, mu_a_ref,
         w0_ref, w2_ref, a0_ref, a2_ref, g2_ref, kk_ref, ka_ref, rk_ref,
         lnw_ref, lnb_ref, o_ref, vo_ref, st_ref, tail_ref) = refs
    c, w, p2 = RWKV_CHUNK, RWKV_WIDTH, 2 * RWKV_HEAD_DIM
    rows = o_ref.shape[0]
    nsub = rows // c

    @pl.when(pl.program_id(1) == 0)
    def _():
        st_ref[...] = jnp.zeros_like(st_ref)
        tail_ref[...] = jnp.zeros_like(tail_ref)

    yield
    def mixed(ref, mu, off):
        x = ref[...]
        wd = x.shape[1]
        prev = _shift_rows(x, 1, tail_ref[:, off:off + wd])
        tail_ref[:, off:off + wd] = x[rows - 8:]
        return x + (prev - x) * mu

    mu_rkv = mu_rkv_ref[...]
    r = mixed(r_ref, mu_rkv[:, :w], 0)
    k = mixed(k_ref, mu_rkv[:, w:2 * w], w)
    v = mixed(v_ref, mu_rkv[:, 2 * w:], 2 * w)
    xg = mixed(xg_ref, mu_g_ref[...], 3 * w)
    xw = mixed(xw_ref, mu_w_ref[...], 3 * w + 256)
    xa = mixed(xa_ref, mu_a_ref[...], 3 * w + 384)
    yield

    w_log = -_softplus(-(w0_ref[...] + _dot(jnp.tanh(xw), w2_ref[...], pa=3, pb=3))) - 0.5
    lw = -jnp.exp(w_log)
    gate = _sigmoid(a0_ref[...] + _dot(xa, a2_ref[...]))
    g = _dot(_sigmoid(xg), g2_ref[...])
    yield
    if has_vres:
        pv = mixed(pv_ref, mu_v_ref[...], 3 * w + 512)
        v = v + (vf_ref[...] - v) * _sigmoid(v0_ref[...] + _dot(pv, v2_ref[...]))
    else:
        vo_ref[...] = v

    lane = _iota((1, p2), 1)
    mask_l = jnp.where(lane < RWKV_HEAD_DIM, 1.0, 0.0)
    mask_r = 1.0 - mask_l
    r2 = _iota((p2, p2), 0)
    c2 = _iota((p2, p2), 1)
    same = (r2 < RWKV_HEAD_DIM) == (c2 < RWKV_HEAD_DIM)
    bd_ones = jnp.where(same, 1.0, 0.0).astype(BF16)

    def head_sum(x, pa):
        return jnp.concatenate(
            [_dot(x[:, i * p2:(i + 1) * p2], bd_ones, pa=pa) for i in range(w // p2)], axis=1)

    kk = k * kk_ref[...]
    kk = kk * jnp.minimum(lax.rsqrt(head_sum(kk * kk, 1)), 1e12)
    k = k * (1.0 + (gate - 1.0) * ka_ref[...])
    kb = kk * gate
    yield

    ri = _iota((rows, rows), 0)
    ci = _iota((rows, rows), 1)
    in_chunk = (ri >= ci) & ((ri - ci) <= (ri & (c - 1)))
    tri = jnp.where(in_chunk, 1.0, 0.0).astype(BF16)
    cum = _dot(tri, lw, pb=2)
    c_ends = [cum[(j + 1) * c - 1:(j + 1) * c] for j in range(nsub)]
    c_last = jnp.concatenate([jnp.broadcast_to(x, (c, w)) for x in c_ends], axis=0)
    g_t = jnp.exp(cum)
    g_inv = jnp.exp(-cum)
    g_end = jnp.exp(c_last - cum)
    g_all = [jnp.exp(x) for x in c_ends]
    a_s = -kk * jnp.exp(cum - lw)
    r_s = r * g_t
    b_h = kb * g_inv
    k_h = k * g_inv
    b_e = kb * g_end
    k_e = k * g_end
    yield

    lower = (r2 >= c2) & same
    strict = (r2 > c2) & same
    eye = jnp.where(r2 == c2, 1.0, 0.0)

    def stack(x):
        return jnp.concatenate([x * mask_l, x * mask_r], axis=0)

    def twice(x):
        return jnp.concatenate([x, x], axis=0)

    def bf(x):
        return x.astype(BF16)

    def mm(a, b):
        return _mm(a, b, ((1,), (0,)))

    npair = w // p2
    pairs = range(nsub * npair)
    sls = [(slice((q // npair) * c, (q // npair + 1) * c), slice((q % npair) * p2, (q % npair + 1) * p2))
           for q in pairs]
    sa = [bf(stack(a_s[s])) for s in sls]
    sr = [stack(r_s[s]) for s in sls]
    sv = [bf(stack(v[s])) for s in sls]
    yield
    gram = [_mm(jnp.concatenate([sa[p], bf(sr[p])], axis=0),
                bf(jnp.concatenate([twice(b_h[sls[p]]), twice(k_h[sls[p]])], axis=0)), NT)
            for p in pairs]
    n = [jnp.where(strict, x[:p2, :p2], 0.0) for x in gram]
    a_ak = [bf(jnp.where(strict, x[:p2, p2:], 0.0)) for x in gram]
    a_rb = [bf(jnp.where(lower, x[p2:, :p2], 0.0)) for x in gram]
    a_rk = [bf(jnp.where(lower, x[p2:, p2:], 0.0)) for x in gram]
    yield
    t = [eye + x for x in n]
    pw = [bf(x) for x in n]
    for _ in range(5):
        pw = [bf(mm(x, x)) for x in pw]
        t = [t[p] + mm(bf(t[p]), pw[p]) for p in pairs]
        yield
    av = [bf(mm(a_ak[p], sv[p])) for p in pairs]
    yield
    au = [bf(mm(bf(t[p]), jnp.concatenate([sa[p], av[p]], axis=1))) for p in pairs]
    yield
    zero = jnp.zeros((p2, p2), BF16)
    z = [mm(jnp.concatenate([jnp.concatenate([a_rb[p], a_rk[p]], axis=1),
                             jnp.concatenate([bf(stack(b_e[sls[p]]).T), bf(stack(k_e[sls[p]]).T)], axis=1)],
                            axis=0),
            jnp.concatenate([au[p], jnp.concatenate([zero, sv[p]], axis=1)], axis=0))
         for p in pairs]
    lhs = [bf(jnp.concatenate([sr[p] + z[p][:p2, :p2], z[p][p2:, :p2]], axis=0)) for p in pairs]
    yield
    g_col = [jnp.sum(eye * g_all[q // npair][:, sls[q][1]], axis=1, keepdims=True) for q in pairs]
    ys = [[None] * npair for _ in range(nsub)]
    for p in range(npair):
        h = st_ref[p]
        for j in range(nsub):
            q = j * npair + p
            hi, lo = _split(h, 2)
            out = mm(jnp.concatenate([lhs[q], lhs[q]], axis=1), jnp.concatenate([hi, lo], axis=0)) + z[q][:, p2:]
            h = out[p2:] + g_col[q] * h
            ys[j][p] = out[:c] + out[c:p2]
        st_ref[p] = h
    yield
    y = jnp.concatenate([jnp.concatenate(row, axis=1) for row in ys], axis=0)

    mean = head_sum(y, 1) * (1.0 / RWKV_HEAD_DIM)
    yc = y - mean
    var = head_sum(yc * yc, 1) * (1.0 / RWKV_HEAD_DIM)
    yn = yc * lax.rsqrt(var + RWKV_LN_EPS) * lnw_ref[...] + lnb_ref[...]
    bonus = head_sum(r * k * rk_ref[...], 1) * v
    o_ref[...] = ((yn + bonus) * g).astype(BF16)
    yield


def _interleave(gens, lead):
    for g in gens:
        next(g)
    for _ in range(lead):
        next(gens[0])
    live = list(gens)
    while live:
        for g in list(live):
            try:
                next(g)
            except StopIteration:
                live.remove(g)


N_RET_IN, N_SSD_IN = 6, 10


def _mixers_kernel(has_vres, n_rw_in, *refs):
    ret_in = refs[:N_RET_IN]
    ssd_in = refs[N_RET_IN:N_RET_IN + N_SSD_IN]
    rw_in = refs[N_RET_IN + N_SSD_IN:N_RET_IN + N_SSD_IN + n_rw_in]
    rest = refs[N_RET_IN + N_SSD_IN + n_rw_in:]
    n_out = 3 if has_vres else 4
    y_ret, y_ssm = rest[0], rest[1]
    rw_out = rest[2:n_out]
    ret_st, ssd_st, ssd_tail, rw_st, rw_tail = rest[n_out:]
    _interleave([_rwkv_body(has_vres, *rw_in, *rw_out, rw_st, rw_tail),
                 _ssd_body(*ssd_in, y_ssm, ssd_st, ssd_tail),
                 _retention_body(*ret_in, y_ret, ret_st)], lead=5)


def _mixers(proj, cos, sin, ssm, rw, vres, b, s):
    rows = RWKV_ROWS
    assert rows == RET_CHUNK == SSM_CHUNK
    nc = s // rows
    w = RWKV_WIDTH
    has_vres = vres is not None

    def row(wd, col):
        return pl.BlockSpec((rows, wd), lambda bi, n, j=col // wd: (bi * nc + n, j))

    def full(x):
        return pl.BlockSpec(x.shape, lambda bi, n: (0, 0))

    def padc(x, n):
        return jnp.pad(x, (0, n - x.shape[0])).reshape(1, n)

    def padr(x, n):
        return jnp.pad(x, ((0, n - x.shape[0]), (0, 0))).astype(BF16)

    tab = pl.BlockSpec((rows, LANE), lambda bi, n: (n, 0))
    ret_args = [proj, proj, proj, proj, cos, sin]
    ret_specs = [row(RET_WIDTH, C_RET + i * RET_WIDTH) for i in range(4)] + [tab, tab]

    conv_w, conv_b, dt_bias, a_log, d_skip, norm_w = ssm
    expand = (jnp.arange(LANE)[:, None] == (jnp.arange(SSM_WIDTH)[None, :] // SSM_HEAD_DIM)).astype(BF16)
    ssd_consts = [jnp.pad(conv_w, ((0, 8 - SSM_CONV), (0, 0))), conv_b.reshape(1, -1), padc(dt_bias, LANE),
                  padc(a_log, LANE), jnp.repeat(d_skip, SSM_HEAD_DIM).reshape(1, SSM_WIDTH),
                  norm_w.reshape(1, -1), expand]
    ssd_args = [proj, proj, proj] + ssd_consts
    ssd_specs = [row(SSM_WIDTH, C_Z), row(SSM_CONV_DIM, C_XBC), row(LANE, C_DT)] + [full(x) for x in ssd_consts]
    assert len(ret_args) == N_RET_IN and len(ssd_args) == N_SSD_IN

    mu, w0, w2, a0, a2, g2, k_k, k_a, r_k, ln_w, ln_b = rw
    o1 = 3 * w + RWKV_DECAY_RANK
    o2 = o1 + RWKV_AAA_RANK
    rw_acts = [proj] * 6
    rw_act_specs = [row(w, C_RKV), row(w, C_RKV + w), row(w, C_RKV + 2 * w), row(256, C_XG),
                    row(LANE, C_XW), row(LANE, C_XA)]
    mus = [mu[:3 * w].reshape(1, -1), mu[o2:].reshape(1, -1), padc(mu[3 * w:o1], LANE), padc(mu[o1:o2], LANE)]
    par = [w0.reshape(1, w), padr(w2, LANE), a0.reshape(1, w), padr(a2, LANE)]
    if has_vres:
        mu_v, v0, v2, v_first = vres
        rw_acts += [proj, v_first]
        rw_act_specs += [row(LANE, C_PV), pl.BlockSpec((rows, w), lambda bi, n: (bi * nc + n, 0))]
        mus.append(padc(mu_v, LANE))
        par += [v0.reshape(1, w), padr(v2, LANE)]
    par += [g2.astype(BF16), k_k.reshape(1, w), k_a.reshape(1, w), r_k.reshape(1, w),
            ln_w.reshape(1, w), ln_b.reshape(1, w)]
    rw_consts = mus + par
    rw_args = rw_acts + rw_consts
    rw_specs = rw_act_specs + [full(x) for x in rw_consts]

    def out_block(wd):
        return pl.BlockSpec((rows, wd), lambda bi, n: (bi * nc + n, 0))

    t = b * s
    out_shape = [jax.ShapeDtypeStruct((t, RET_WIDTH), BF16), jax.ShapeDtypeStruct((t, SSM_WIDTH), BF16),
                 jax.ShapeDtypeStruct((t, w), BF16)]
    out_specs = [out_block(RET_WIDTH), out_block(SSM_WIDTH), out_block(w)]
    if not has_vres:
        out_shape.append(jax.ShapeDtypeStruct((t, w), F32))
        out_specs.append(out_block(w))
    return pl.pallas_call(
        functools.partial(_mixers_kernel, has_vres, len(rw_args)),
        out_shape=tuple(out_shape),
        grid=(b, nc),
        in_specs=ret_specs + ssd_specs + rw_specs,
        out_specs=tuple(out_specs),
        scratch_shapes=[pltpu.VMEM((RET_HEADS, RET_HEAD_DIM, RET_HEAD_DIM), F32),
                        pltpu.VMEM((SSM_HEADS // 2, SSM_STATE, 2 * SSM_HEAD_DIM), F32),
                        pltpu.VMEM((8, SSM_CONV_DIM), F32),
                        pltpu.VMEM((RWKV_HEADS // 2, 2 * RWKV_HEAD_DIM, 2 * RWKV_HEAD_DIM), F32),
                        pltpu.VMEM((8, 3 * w + 256 + 3 * LANE), F32)],
        compiler_params=_cparams(("parallel", "arbitrary")),
        name="mixers",
    )(*ret_args, *ssd_args, *rw_args)


def _out_proj_kernel(x_ref, yr_ref, ys_ref, yw_ref, wr_ref, ws_ref, ww_ref, o_ref):
    acc = x_ref[...]
    acc = acc + jnp.dot(yr_ref[...], wr_ref[...], preferred_element_type=F32)
    acc = acc + jnp.dot(ys_ref[...], ws_ref[...], preferred_element_type=F32)
    acc = acc + jnp.dot(yw_ref[...], ww_ref[...], preferred_element_type=F32)
    o_ref[...] = acc


def _out_proj(x, y_ret, y_ssm, y_rwkv, w_out, tm=1024, tn=1024):
    t, d = x.shape
    wr = w_out[:RET_WIDTH].astype(BF16)
    ws = w_out[RET_WIDTH:RET_WIDTH + SSM_WIDTH].astype(BF16)
    ww = w_out[RET_WIDTH + SSM_WIDTH:].astype(BF16)

    def rows(wd):
        return pl.BlockSpec((tm, wd), lambda i, j: (i, 0))

    def cols(k):
        return pl.BlockSpec((k, tn), lambda i, j: (0, j))

    return pl.pallas_call(
        _out_proj_kernel,
        out_shape=jax.ShapeDtypeStruct((t, d), F32),
        grid=(t // tm, d // tn),
        in_specs=[pl.BlockSpec((tm, tn), lambda i, j: (i, j)),
                  rows(RET_WIDTH), rows(SSM_WIDTH), rows(RWKV_WIDTH),
                  cols(RET_WIDTH), cols(SSM_WIDTH), cols(RWKV_WIDTH)],
        out_specs=pl.BlockSpec((tm, tn), lambda i, j: (i, j)),
        compiler_params=_cparams(("parallel", "parallel")),
        name="out_proj",
    )(x, y_ret, y_ssm, y_rwkv, wr, ws, ww)


def _ffn_kernel(has_final, *refs):
    if has_final:
        x_ref, nw_ref, wg_ref, wu_ref, wd_ref, fw_ref, o_ref, h_ref = refs
    else:
        x_ref, nw_ref, wg_ref, wu_ref, wd_ref, o_ref, h_ref = refs
    j = pl.program_id(1)

    @pl.when(j == 0)
    def _():
        x = x_ref[...]
        ms = jnp.mean(x * x, axis=-1, keepdims=True)
        h_ref[...] = (x * lax.rsqrt(ms + NORM_EPS) * nw_ref[...]).astype(BF16)
        o_ref[...] = x

    h = h_ref[...]
    gt = jnp.dot(h, wg_ref[...], preferred_element_type=F32)
    up = jnp.dot(h, wu_ref[...], preferred_element_type=F32)
    act = (_silu(gt) * up).astype(BF16)
    o_ref[...] += jnp.dot(act, wd_ref[...], preferred_element_type=F32)

    if has_final:
        @pl.when(j == pl.num_programs(1) - 1)
        def _():
            y = o_ref[...]
            ms = jnp.mean(y * y, axis=-1, keepdims=True)
            o_ref[...] = y * lax.rsqrt(ms + NORM_EPS) * fw_ref[...]


def _ffn(x, nw, wg, wu, wd, final_w=None, tm=1024, tf=512):
    t, d = x.shape
    f = wg.shape[1]
    has_final = final_w is not None
    in_specs = [pl.BlockSpec((tm, d), lambda i, j: (i, 0)),
                pl.BlockSpec((1, d), lambda i, j: (0, 0)),
                pl.BlockSpec((d, tf), lambda i, j: (0, j)),
                pl.BlockSpec((d, tf), lambda i, j: (0, j)),
                pl.BlockSpec((tf, d), lambda i, j: (j, 0))]
    args = [x, nw.reshape(1, d), wg.astype(BF16), wu.astype(BF16), wd.astype(BF16)]
    if has_final:
        in_specs.append(pl.BlockSpec((1, d), lambda i, j: (0, 0)))
        args.append(final_w.reshape(1, d))
    return pl.pallas_call(
        functools.partial(_ffn_kernel, has_final),
        out_shape=jax.ShapeDtypeStruct((t, d), F32),
        grid=(t // tm, f // tf),
        in_specs=in_specs,
        out_specs=pl.BlockSpec((tm, d), lambda i, j: (i, 0)),
        scratch_shapes=[pltpu.VMEM((tm, d), BF16)],
        compiler_params=_cparams(("parallel", "arbitrary")),
        name="ffn",
    )(*args)


def _pack_w_in(w_in):
    d = w_in.shape[0]
    o_z = RET_COLS
    o_xbc = o_z + SSM_WIDTH
    o_dt = o_xbc + SSM_CONV_DIM
    o_rkv = RET_COLS + SSM_COLS
    o_xw = o_rkv + 3 * RWKV_WIDTH
    o_xa = o_xw + RWKV_DECAY_RANK
    o_xg = o_xa + RWKV_AAA_RANK
    o_pv = IN_COLS

    def seg(lo, hi, width):
        blk = w_in[:, lo:hi]
        return jnp.pad(blk, ((0, 0), (0, width - (hi - lo))))

    pv_hi = w_in.shape[1]
    parts = [seg(0, o_z, RET_COLS), seg(o_z, o_xbc, SSM_WIDTH), seg(o_xbc, o_dt, SSM_CONV_DIM),
             seg(o_rkv, o_xw, 3 * RWKV_WIDTH), seg(o_xg, o_pv, RWKV_GATE_RANK),
             seg(o_dt, o_rkv, LANE), seg(o_xw, o_xa, LANE), seg(o_xa, o_xg, LANE)]
    if pv_hi > o_pv:
        parts.append(seg(o_pv, pv_hi, LANE))
    else:
        parts.append(jnp.zeros((d, LANE), w_in.dtype))
    return jnp.concatenate(parts, axis=1).astype(BF16)


def kernel(x, norm_mix_w, w_in_first, w_in_rest, ssm_conv_w, ssm_conv_b, ssm_dt_bias, ssm_a_log, ssm_d, ssm_norm_w, rwkv_mu, rwkv_mu_v, rwkv_w0, rwkv_w2, rwkv_a0, rwkv_a2, rwkv_v0, rwkv_v2, rwkv_g2, rwkv_k_k, rwkv_k_a, rwkv_r_k, rwkv_ln_w, rwkv_ln_b, w_out, norm_ffn_w, ffn_w_gate, ffn_w_up, ffn_w_down, final_norm_w):
    b, s, d = x.shape
    depth = norm_mix_w.shape[0]
    xf = x.reshape(b * s, d)
    cos, sin = _rope_tables(s)
    v_first = None
    for l in range(depth):
        w_in = w_in_first if l == 0 else w_in_rest[l - 1]
        proj = _norm_matmul(xf, norm_mix_w[l], _pack_w_in(w_in))
        vres = None if l == 0 else (rwkv_mu_v[l - 1], rwkv_v0[l - 1], rwkv_v2[l - 1], v_first)
        ssm = (ssm_conv_w[l], ssm_conv_b[l], ssm_dt_bias[l], ssm_a_log[l], ssm_d[l], ssm_norm_w[l])
        rw = (rwkv_mu[l], rwkv_w0[l], rwkv_w2[l], rwkv_a0[l], rwkv_a2[l], rwkv_g2[l], rwkv_k_k[l], rwkv_k_a[l],
              rwkv_r_k[l], rwkv_ln_w[l], rwkv_ln_b[l])
        out = _mixers(proj, cos, sin, ssm, rw, vres, b, s)
        y_ret, y_ssm, y_rwkv = out[:3]
        if l == 0:
            v_first = out[3]
        xf = _out_proj(xf, y_ret, y_ssm, y_rwkv, w_out[l])
        xf = _ffn(xf, norm_ffn_w[l], ffn_w_gate[l], ffn_w_up[l], ffn_w_down[l],
                  final_norm_w if l == depth - 1 else None)
    return xf.reshape(b, s, d)
```

```python
import functools
import math

import jax
import jax.numpy as jnp
from jax import lax
from jax.experimental import pallas as pl
from jax.experimental.pallas import tpu as pltpu

F32 = jnp.float32
BF16 = jnp.bfloat16

D_MODEL = 2048
NORM_EPS = 1e-6
RET_HEADS, RET_HEAD_DIM = 4, 128
RET_WIDTH = RET_HEADS * RET_HEAD_DIM
RET_GN_EPS = 1e-6
ROPE_BASE = 10000.0
RET_CHUNK = 128
SSM_HEADS, SSM_HEAD_DIM = 16, 64
SSM_WIDTH = SSM_HEADS * SSM_HEAD_DIM
SSM_GROUPS, SSM_STATE, SSM_CONV = 2, 128, 4
SSM_CONV_DIM = SSM_WIDTH + 2 * SSM_GROUPS * SSM_STATE
SSM_NORM_EPS = 1e-5
SSM_CHUNK = 128
RWKV_HEADS, RWKV_HEAD_DIM = 8, 64
RWKV_WIDTH = RWKV_HEADS * RWKV_HEAD_DIM
RWKV_DECAY_RANK, RWKV_AAA_RANK, RWKV_MV_RANK, RWKV_GATE_RANK = 96, 96, 64, 256
RWKV_LN_EPS = 64e-5
RWKV_CHUNK = 64
RWKV_ROWS = 128
RET_COLS = 4 * RET_WIDTH
SSM_COLS = SSM_WIDTH + SSM_CONV_DIM + SSM_HEADS
RWKV_COLS = 3 * RWKV_WIDTH + RWKV_DECAY_RANK + RWKV_AAA_RANK + RWKV_GATE_RANK
IN_COLS = RET_COLS + SSM_COLS + RWKV_COLS

LANE = 128
VMEM_LIMIT = 56 * 1024 * 1024

C_RET = 0
C_Z = 2048
C_XBC = 3072
C_RKV = 4608
C_XG = 6144
C_DT = 6400
C_XW = 6528
C_XA = 6656
C_PV = 6784
IN_PAD = 6912


def _cparams(sem):
    return pltpu.CompilerParams(dimension_semantics=sem, vmem_limit_bytes=VMEM_LIMIT)


def _split(x, n):
    parts = []
    rem = x
    for i in range(n):
        p = rem.astype(BF16)
        parts.append(p)
        if i + 1 < n:
            rem = rem - p.astype(F32)
    return parts


def _mm(a, b, dims):
    return lax.dot_general(a, b, (dims, ((), ())), preferred_element_type=F32)


def _dot(a, b, pa=1, pb=1, dims=((1,), (0,))):
    ap = _split(a, pa) if a.dtype != BF16 else [a]
    bp = _split(b, pb) if b.dtype != BF16 else [b]
    order = max(len(ap), len(bp))
    acc = None
    for i, x in enumerate(ap):
        for j, y in enumerate(bp):
            if i + j < order:
                t = _mm(x, y, dims)
                acc = t if acc is None else acc + t
    return acc


LOG2E = math.log2(math.e)
NT = ((1,), (1,))
TN = ((0,), (0,))


def _sigmoid(x):
    return 1.0 / (1.0 + jnp.exp(-x))


def _silu(x):
    return x * _sigmoid(x)


def _softplus(x):
    return jnp.maximum(x, 0.0) + jnp.log(1.0 + jnp.exp(-jnp.abs(x)))


def _iota(shape, dim):
    return lax.broadcasted_iota(jnp.int32, shape, dim)


def _shift_rows(x, s, tail):
    xs = pltpu.roll(x, s, axis=0)
    tl = pltpu.roll(tail, s, axis=0)
    head = jnp.where(_iota(tl.shape, 0) < s, tl, xs[:8])
    return jnp.concatenate([head, xs[8:]], axis=0)


def _norm_matmul_kernel(x_ref, nw_ref, w_ref, o_ref, h_ref):
    @pl.when(pl.program_id(1) == 0)
    def _():
        x = x_ref[...]
        ms = jnp.mean(x * x, axis=-1, keepdims=True)
        h_ref[...] = (x * lax.rsqrt(ms + NORM_EPS) * nw_ref[...]).astype(BF16)

    o_ref[...] = jnp.dot(h_ref[...], w_ref[...], preferred_element_type=F32)


def _norm_matmul(x, nw, w, tm=1024, tn=768):
    t, d = x.shape
    n = w.shape[1]
    return pl.pallas_call(
        _norm_matmul_kernel,
        out_shape=jax.ShapeDtypeStruct((t, n), F32),
        grid=(t // tm, n // tn),
        in_specs=[pl.BlockSpec((tm, d), lambda i, j: (i, 0)),
                  pl.BlockSpec((1, d), lambda i, j: (0, 0)),
                  pl.BlockSpec((d, tn), lambda i, j: (0, j))],
        out_specs=pl.BlockSpec((tm, tn), lambda i, j: (i, j)),
        scratch_shapes=[pltpu.VMEM((tm, d), BF16)],
        compiler_params=_cparams(("parallel", "arbitrary")),
        name="norm_in_proj",
    )(x, nw.reshape(1, d), w)


def _rope_kernel(cos_ref, sin_ref):
    rows = cos_ref.shape[0]
    half = RET_HEAD_DIM // 2
    pos = (pl.program_id(0) * rows + _iota((rows, LANE), 0)).astype(F32)
    lane = _iota((rows, LANE), 1)
    j = jnp.where(lane >= half, lane - half, lane).astype(F32)
    inv_freq = jnp.exp(j * (-math.log(ROPE_BASE) / half))
    ang = pos * inv_freq
    cos_ref[...] = jnp.cos(ang)
    sin_ref[...] = jnp.where(lane >= half, jnp.sin(ang), -jnp.sin(ang))


def _rope_tables(s, rows=512):
    return pl.pallas_call(
        _rope_kernel,
        out_shape=(jax.ShapeDtypeStruct((s, LANE), F32), jax.ShapeDtypeStruct((s, LANE), F32)),
        grid=(s // rows,),
        out_specs=(pl.BlockSpec((rows, LANE), lambda i: (i, 0)),
                   pl.BlockSpec((rows, LANE), lambda i: (i, 0))),
        compiler_params=_cparams(("parallel",)),
        name="rope_tables",
    )()


def _retention_body(q_ref, k_ref, v_ref, g_ref, cos_ref, sin_ref, o_ref, st_ref):
    c, dh = RET_CHUNK, RET_HEAD_DIM

    @pl.when(pl.program_id(1) == 0)
    def _():
        st_ref[...] = jnp.zeros_like(st_ref)

    yield
    cos = cos_ref[...]
    sin = sin_ref[...]
    ri = _iota((c, c), 0)
    ci = _iota((c, c), 1)
    rel = (ri - ci).astype(F32)
    causal = ri >= ci
    tcol = _iota((c, 1), 0).astype(F32)
    for h in range(RET_HEADS):
        lg = math.log1p(-(2.0 ** (-5 - h)))
        sl = slice(h * dh, (h + 1) * dh)
        q = q_ref[:, sl]
        k = k_ref[:, sl]
        v = v_ref[:, sl]
        qr = q * cos + pltpu.roll(q, dh // 2, axis=1) * sin
        kr = (k * cos + pltpu.roll(k, dh // 2, axis=1) * sin) * (dh ** -0.5)
        decay = jnp.where(causal, jnp.exp2(jnp.maximum(rel, 0.0) * (lg * LOG2E)), 0.0)
        vb = v.astype(BF16)
        scores = _dot(qr, kr, dims=NT) * decay
        yield
        inner = _dot(scores, vb)
        prev = st_ref[h]
        cross = _dot(qr * jnp.exp((tcol + 1.0) * lg), prev)
        kz = kr * jnp.exp((c - 1.0 - tcol) * lg)
        st_ref[h] = prev * math.exp(c * lg) + _dot(kz.T, vb)
        yield
        y = inner + cross
        yc = y - jnp.mean(y, axis=-1, keepdims=True)
        yn = yc * lax.rsqrt(jnp.mean(yc * yc, axis=-1, keepdims=True) + RET_GN_EPS)
        o_ref[:, sl] = (yn * _silu(g_ref[:, sl])).astype(BF16)
        yield


def _ssd_body(z_ref, xbc_ref, dt_ref, cw_ref, cb_ref, dtb_ref, alog_ref, dsk_ref, nw_ref, ex_ref,
              o_ref, st_ref, tail_ref):
    c, p2, nst = SSM_CHUNK, 2 * SSM_HEAD_DIM, SSM_STATE
    gn = SSM_GROUPS * SSM_STATE

    @pl.when(pl.program_id(1) == 0)
    def _():
        st_ref[...] = jnp.zeros_like(st_ref)
        tail_ref[...] = jnp.zeros_like(tail_ref)

    yield
    raw = xbc_ref[...]
    tail = tail_ref[...]
    cw = cw_ref[...]
    conv = raw * cw[3:4] + cb_ref[...]
    for s in range(1, SSM_CONV):
        conv = conv + _shift_rows(raw, s, tail) * cw[3 - s:4 - s]
    tail_ref[...] = raw[c - 8:]
    yield
    xbc = _silu(conv)
    xs = xbc[:, :SSM_WIDTH]
    bm = xbc[:, SSM_WIDTH:SSM_WIDTH + gn]
    cm = xbc[:, SSM_WIDTH + gn:]
    yield

    dt = _softplus(dt_ref[...] + dtb_ref[...])
    a = dt * (-LOG2E * jnp.exp(alog_ref[...]))
    ri = _iota((c, c), 0)
    ci = _iota((c, c), 1)
    causal = ri >= ci
    tri = jnp.where(causal, 1.0, 0.0).astype(BF16)
    acum = _dot(tri, a, pb=3)
    acum_t = acum.T
    ex = ex_ref[...]
    dt_e = _dot(dt, ex, pa=3)
    acum_e = _dot(acum, ex, pa=3)
    a_last = acum_e[c - 1:c]
    from_start = jnp.exp2(acum_e)
    to_end = jnp.exp2(a_last - acum_e)
    chunk_decay = jnp.exp2(a_last)
    x_dt = xs * dt_e
    lane = _iota((1, p2), 1)
    mask_l = jnp.where(lane < SSM_HEAD_DIM, 1.0, 0.0)
    mask_r = 1.0 - mask_l
    yield

    pairs_per_group = SSM_HEADS // SSM_GROUPS // 2
    ys = []
    for g in range(SSM_GROUPS):
        bg = bm[:, g * nst:(g + 1) * nst]
        cg = cm[:, g * nst:(g + 1) * nst]
        cbm = _dot(cg, bg, dims=NT)
        bgt = bg.T.astype(BF16)
        cgb = cg.astype(BF16)
        for pp in range(pairs_per_group):
            p = g * pairs_per_group + pp
            sl = slice(p * p2, (p + 1) * p2)
            xp = x_dt[:, sl]
            y = None
            for hh, msk in ((2 * p, mask_l), (2 * p + 1, mask_r)):
                seg = acum[:, hh:hh + 1] - acum_t[hh:hh + 1, :]
                m = cbm * jnp.where(causal, jnp.exp2(jnp.minimum(seg, 0.0)), 0.0)
                t = _dot(m, xp * msk)
                y = t if y is None else y + t
            prev = st_ref[p]
            y = y + _dot(cgb, prev) * from_start[:, sl]
            st_ref[p] = prev * chunk_decay[:, sl] + _dot(bgt, xp * to_end[:, sl])
            ys.append(y + xs[:, sl] * dsk_ref[:, sl])
            yield
    y = jnp.concatenate(ys, axis=1) * _silu(z_ref[...])
    gw = SSM_WIDTH // SSM_GROUPS
    nw = nw_ref[...]
    for g in range(SSM_GROUPS):
        sl = slice(g * gw, (g + 1) * gw)
        yg = y[:, sl]
        ms = jnp.mean(yg * yg, axis=-1, keepdims=True)
        o_ref[:, sl] = (yg * lax.rsqrt(ms + SSM_NORM_EPS) * nw[:, sl]).astype(BF16)
        yield


def _rwkv_body(has_vres, *refs):
    if has_vres:
        (r_ref, k_ref, v_ref, xg_ref, xw_ref, xa_ref, pv_ref, vf_ref,
         mu_rkv_ref, mu_g_ref, mu_w_ref, mu_a_ref, mu_v_ref,
         w0_ref, w2_ref, a0_ref, a2_ref, v0_ref, v2_ref, g2_ref, kk_ref, ka_ref, rk_ref,
         lnw_ref, lnb_ref, o_ref, st_ref, tail_ref) = refs
    else:
        (r_ref, k_ref, v_ref, xg_ref, xw_ref, xa_ref,
         mu_rkv_ref, mu_g_ref, mu_w_ref, mu_a_ref,
         w0_ref, w2_ref, a0_ref, a2_ref, g2_ref, kk_ref, ka_ref, rk_ref,
         lnw_ref, lnb_ref, o_ref, vo_ref, st_ref, tail_ref) = refs
    c, w, p2 = RWKV_CHUNK, RWKV_WIDTH, 2 * RWKV_HEAD_DIM
    rows = o_ref.shape[0]
    nsub = rows // c

    @pl.when(pl.program_id(1) == 0)
    def _():
        st_ref[...] = jnp.zeros_like(st_ref)
        tail_ref[...] = jnp.zeros_like(tail_ref)

    yield
    def mixed(ref, mu, off):
        x = ref[...]
        wd = x.shape[1]
        prev = _shift_rows(x, 1, tail_ref[:, off:off + wd])
        tail_ref[:, off:off + wd] = x[rows - 8:]
        return x + (prev - x) * mu

    mu_rkv = mu_rkv_ref[...]
    r = mixed(r_ref, mu_rkv[:, :w], 0)
    k = mixed(k_ref, mu_rkv[:, w:2 * w], w)
    v = mixed(v_ref, mu_rkv[:, 2 * w:], 2 * w)
    xg = mixed(xg_ref, mu_g_ref[...], 3 * w)
    xw = mixed(xw_ref, mu_w_ref[...], 3 * w + 256)
    xa = mixed(xa_ref, mu_a_ref[...], 3 * w + 384)
    yield

    w_log = -_softplus(-(w0_ref[...] + _dot(jnp.tanh(xw), w2_ref[...], pa=3, pb=3))) - 0.5
    lw = -LOG2E * jnp.exp(w_log)
    gate = _sigmoid(a0_ref[...] + _dot(xa, a2_ref[...]))
    g = _dot(_sigmoid(xg), g2_ref[...])
    yield
    if has_vres:
        pv = mixed(pv_ref, mu_v_ref[...], 3 * w + 512)
        v = v + (vf_ref[...] - v) * _sigmoid(v0_ref[...] + _dot(pv, v2_ref[...]))
    else:
        vo_ref[...] = v

    lane = _iota((1, p2), 1)
    mask_l = jnp.where(lane < RWKV_HEAD_DIM, 1.0, 0.0)
    mask_r = 1.0 - mask_l
    r2 = _iota((p2, p2), 0)
    c2 = _iota((p2, p2), 1)
    same = (r2 < RWKV_HEAD_DIM) == (c2 < RWKV_HEAD_DIM)
    bd_ones = jnp.where(same, 1.0, 0.0).astype(BF16)

    def head_sum(x, pa):
        return jnp.concatenate(
            [_dot(x[:, i * p2:(i + 1) * p2], bd_ones, pa=pa) for i in range(w // p2)], axis=1)

    kk = k * kk_ref[...]
    kk = kk * jnp.minimum(lax.rsqrt(head_sum(kk * kk, 1)), 1e12)
    k = k * (1.0 + (gate - 1.0) * ka_ref[...])
    kb = kk * gate
    yield

    ri = _iota((rows, rows), 0)
    ci = _iota((rows, rows), 1)
    in_chunk = (ri >= ci) & ((ri - ci) <= (ri & (c - 1)))
    tri = jnp.where(in_chunk, 1.0, 0.0).astype(BF16)
    cum = _dot(tri, lw, pb=2)
    c_ends = [cum[(j + 1) * c - 1:(j + 1) * c] for j in range(nsub)]
    c_last = jnp.concatenate([jnp.broadcast_to(x, (c, w)) for x in c_ends], axis=0)
    g_t = jnp.exp2(cum)
    g_inv = jnp.exp2(-cum)
    g_end = jnp.exp2(c_last - cum)
    g_all = [jnp.exp2(x) for x in c_ends]
    a_s = -kk * jnp.exp2(cum - lw)
    r_s = r * g_t
    b_h = kb * g_inv
    k_h = k * g_inv
    b_e = kb * g_end
    k_e = k * g_end
    yield

    lower = (r2 >= c2) & same
    strict = (r2 > c2) & same
    eye = jnp.where(r2 == c2, 1.0, 0.0)

    def stack(x):
        return jnp.concatenate([x * mask_l, x * mask_r], axis=0)

    def twice(x):
        return jnp.concatenate([x, x], axis=0)

    def bf(x):
        return x.astype(BF16)

    def mm(a, b):
        return _mm(a, b, ((1,), (0,)))

    npair = w // p2
    pairs = range(nsub * npair)
    sls = [(slice((q // npair) * c, (q // npair + 1) * c), slice((q % npair) * p2, (q % npair + 1) * p2))
           for q in pairs]
    sa = [bf(stack(a_s[s])) for s in sls]
    sr = [stack(r_s[s]) for s in sls]
    sv = [bf(stack(v[s])) for s in sls]
    yield
    gram = [_mm(jnp.concatenate([sa[p], bf(sr[p])], axis=0),
                bf(jnp.concatenate([twice(b_h[sls[p]]), twice(k_h[sls[p]])], axis=0)), NT)
            for p in pairs]
    n = [jnp.where(strict, x[:p2, :p2], 0.0) for x in gram]
    a_ak = [bf(jnp.where(strict, x[:p2, p2:], 0.0)) for x in gram]
    a_rb = [bf(jnp.where(lower, x[p2:, :p2], 0.0)) for x in gram]
    a_rk = [bf(jnp.where(lower, x[p2:, p2:], 0.0)) for x in gram]
    yield
    t = [eye + x for x in n]
    pw = [bf(x) for x in n]
    for _ in range(5):
        pw = [bf(mm(x, x)) for x in pw]
        t = [t[p] + mm(bf(t[p]), pw[p]) for p in pairs]
        yield
    av = [bf(mm(a_ak[p], sv[p])) for p in pairs]
    yield
    au = [bf(mm(bf(t[p]), jnp.concatenate([sa[p], av[p]], axis=1))) for p in pairs]
    yield
    zero = jnp.zeros((p2, p2), BF16)
    z = [mm(jnp.concatenate([jnp.concatenate([a_rb[p], a_rk[p]], axis=1),
                             jnp.concatenate([bf(stack(b_e[sls[p]]).T), bf(stack(k_e[sls[p]]).T)], axis=1)],
                            axis=0),
            jnp.concatenate([au[p], jnp.concatenate([zero, sv[p]], axis=1)], axis=0))
         for p in pairs]
    lhs = [bf(jnp.concatenate([sr[p] + z[p][:p2, :p2], z[p][p2:, :p2]], axis=0)) for p in pairs]
    yield
    g_col = [jnp.sum(eye * g_all[q // npair][:, sls[q][1]], axis=1, keepdims=True) for q in pairs]
    ys = [[None] * npair for _ in range(nsub)]
    for p in range(npair):
        h = st_ref[p]
        for j in range(nsub):
            q = j * npair + p
            hi, lo = _split(h, 2)
            out = mm(jnp.concatenate([lhs[q], lhs[q]], axis=1), jnp.concatenate([hi, lo], axis=0)) + z[q][:, p2:]
            h = out[p2:] + g_col[q] * h
            ys[j][p] = out[:c] + out[c:p2]
        st_ref[p] = h
    yield
    y = jnp.concatenate([jnp.concatenate(row, axis=1) for row in ys], axis=0)

    mean = head_sum(y, 1) * (1.0 / RWKV_HEAD_DIM)
    yc = y - mean
    var = head_sum(yc * yc, 1) * (1.0 / RWKV_HEAD_DIM)
    yn = yc * lax.rsqrt(var + RWKV_LN_EPS) * lnw_ref[...] + lnb_ref[...]
    bonus = head_sum(r * k * rk_ref[...], 1) * v
    o_ref[...] = ((yn + bonus) * g).astype(BF16)
    yield


def _interleave(gens, lead):
    for g in gens:
        next(g)
    for _ in range(lead):
        next(gens[0])
    live = list(gens)
    while live:
        for g in list(live):
            try:
                next(g)
            except StopIteration:
                live.remove(g)


N_RET_IN, N_SSD_IN = 6, 10


def _mixers_kernel(has_vres, n_rw_in, *refs):
    ret_in = refs[:N_RET_IN]
    ssd_in = refs[N_RET_IN:N_RET_IN + N_SSD_IN]
    rw_in = refs[N_RET_IN + N_SSD_IN:N_RET_IN + N_SSD_IN + n_rw_in]
    rest = refs[N_RET_IN + N_SSD_IN + n_rw_in:]
    n_out = 3 if has_vres else 4
    y_ret, y_ssm = rest[0], rest[1]
    rw_out = rest[2:n_out]
    ret_st, ssd_st, ssd_tail, rw_st, rw_tail = rest[n_out:]
    _interleave([_rwkv_body(has_vres, *rw_in, *rw_out, rw_st, rw_tail),
                 _ssd_body(*ssd_in, y_ssm, ssd_st, ssd_tail),
                 _retention_body(*ret_in, y_ret, ret_st)], lead=5)


def _mixers(proj, cos, sin, ssm, rw, vres, b, s):
    rows = RWKV_ROWS
    assert rows == RET_CHUNK == SSM_CHUNK
    nc = s // rows
    w = RWKV_WIDTH
    has_vres = vres is not None

    def row(wd, col):
        return pl.BlockSpec((rows, wd), lambda bi, n, j=col // wd: (bi * nc + n, j))

    def full(x):
        return pl.BlockSpec(x.shape, lambda bi, n: (0, 0))

    def padc(x, n):
        return jnp.pad(x, (0, n - x.shape[0])).reshape(1, n)

    def padr(x, n):
        return jnp.pad(x, ((0, n - x.shape[0]), (0, 0))).astype(BF16)

    tab = pl.BlockSpec((rows, LANE), lambda bi, n: (n, 0))
    ret_args = [proj, proj, proj, proj, cos, sin]
    ret_specs = [row(RET_WIDTH, C_RET + i * RET_WIDTH) for i in range(4)] + [tab, tab]

    conv_w, conv_b, dt_bias, a_log, d_skip, norm_w = ssm
    expand = (jnp.arange(LANE)[:, None] == (jnp.arange(SSM_WIDTH)[None, :] // SSM_HEAD_DIM)).astype(BF16)
    ssd_consts = [jnp.pad(conv_w, ((0, 8 - SSM_CONV), (0, 0))), conv_b.reshape(1, -1), padc(dt_bias, LANE),
                  padc(a_log, LANE), jnp.repeat(d_skip, SSM_HEAD_DIM).reshape(1, SSM_WIDTH),
                  norm_w.reshape(1, -1), expand]
    ssd_args = [proj, proj, proj] + ssd_consts
    ssd_specs = [row(SSM_WIDTH, C_Z), row(SSM_CONV_DIM, C_XBC), row(LANE, C_DT)] + [full(x) for x in ssd_consts]
    assert len(ret_args) == N_RET_IN and len(ssd_args) == N_SSD_IN

    mu, w0, w2, a0, a2, g2, k_k, k_a, r_k, ln_w, ln_b = rw
    o1 = 3 * w + RWKV_DECAY_RANK
    o2 = o1 + RWKV_AAA_RANK
    rw_acts = [proj] * 6
    rw_act_specs = [row(w, C_RKV), row(w, C_RKV + w), row(w, C_RKV + 2 * w), row(256, C_XG),
                    row(LANE, C_XW), row(LANE, C_XA)]
    mus = [mu[:3 * w].reshape(1, -1), mu[o2:].reshape(1, -1), padc(mu[3 * w:o1], LANE), padc(mu[o1:o2], LANE)]
    par = [w0.reshape(1, w), padr(w2, LANE), a0.reshape(1, w), padr(a2, LANE)]
    if has_vres:
        mu_v, v0, v2, v_first = vres
        rw_acts += [proj, v_first]
        rw_act_specs += [row(LANE, C_PV), pl.BlockSpec((rows, w), lambda bi, n: (bi * nc + n, 0))]
        mus.append(padc(mu_v, LANE))
        par += [v0.reshape(1, w), padr(v2, LANE)]
    par += [g2.astype(BF16), k_k.reshape(1, w), k_a.reshape(1, w), r_k.reshape(1, w),
            ln_w.reshape(1, w), ln_b.reshape(1, w)]
    rw_consts = mus + par
    rw_args = rw_acts + rw_consts
    rw_specs = rw_act_specs + [full(x) for x in rw_consts]

    def out_block(wd):
        return pl.BlockSpec((rows, wd), lambda bi, n: (bi * nc + n, 0))

    t = b * s
    out_shape = [jax.ShapeDtypeStruct((t, RET_WIDTH), BF16), jax.ShapeDtypeStruct((t, SSM_WIDTH), BF16),
                 jax.ShapeDtypeStruct((t, w), BF16)]
    out_specs = [out_block(RET_WIDTH), out_block(SSM_WIDTH), out_block(w)]
    if not has_vres:
        out_shape.append(jax.ShapeDtypeStruct((t, w), F32))
        out_specs.append(out_block(w))
    return pl.pallas_call(
        functools.partial(_mixers_kernel, has_vres, len(rw_args)),
        out_shape=tuple(out_shape),
        grid=(b, nc),
        in_specs=ret_specs + ssd_specs + rw_specs,
        out_specs=tuple(out_specs),
        scratch_shapes=[pltpu.VMEM((RET_HEADS, RET_HEAD_DIM, RET_HEAD_DIM), F32),
                        pltpu.VMEM((SSM_HEADS // 2, SSM_STATE, 2 * SSM_HEAD_DIM), F32),
                        pltpu.VMEM((8, SSM_CONV_DIM), F32),
                        pltpu.VMEM((RWKV_HEADS // 2, 2 * RWKV_HEAD_DIM, 2 * RWKV_HEAD_DIM), F32),
                        pltpu.VMEM((8, 3 * w + 256 + 3 * LANE), F32)],
        compiler_params=_cparams(("parallel", "arbitrary")),
        name="mixers",
    )(*ret_args, *ssd_args, *rw_args)


def _out_proj_kernel(x_ref, yr_ref, ys_ref, yw_ref, wr_ref, ws_ref, ww_ref, o_ref):
    acc = x_ref[...]
    acc = acc + jnp.dot(yr_ref[...], wr_ref[...], preferred_element_type=F32)
    acc = acc + jnp.dot(ys_ref[...], ws_ref[...], preferred_element_type=F32)
    acc = acc + jnp.dot(yw_ref[...], ww_ref[...], preferred_element_type=F32)
    o_ref[...] = acc


def _out_proj(x, y_ret, y_ssm, y_rwkv, w_out, tm=512, tn=D_MODEL):
    t, d = x.shape
    wr = w_out[:RET_WIDTH].astype(BF16)
    ws = w_out[RET_WIDTH:RET_WIDTH + SSM_WIDTH].astype(BF16)
    ww = w_out[RET_WIDTH + SSM_WIDTH:].astype(BF16)

    def rows(wd):
        return pl.BlockSpec((tm, wd), lambda i, j: (i, 0))

    def cols(k):
        return pl.BlockSpec((k, tn), lambda i, j: (0, j))

    return pl.pallas_call(
        _out_proj_kernel,
        out_shape=jax.ShapeDtypeStruct((t, d), F32),
        grid=(t // tm, d // tn),
        in_specs=[pl.BlockSpec((tm, tn), lambda i, j: (i, j)),
                  rows(RET_WIDTH), rows(SSM_WIDTH), rows(RWKV_WIDTH),
                  cols(RET_WIDTH), cols(SSM_WIDTH), cols(RWKV_WIDTH)],
        out_specs=pl.BlockSpec((tm, tn), lambda i, j: (i, j)),
        compiler_params=_cparams(("parallel", "parallel")),
        name="out_proj",
    )(x, y_ret, y_ssm, y_rwkv, wr, ws, ww)


def _ffn_kernel(has_final, *refs):
    if has_final:
        x_ref, nw_ref, wg_ref, wu_ref, wd_ref, fw_ref, o_ref, h_ref = refs
    else:
        x_ref, nw_ref, wg_ref, wu_ref, wd_ref, o_ref, h_ref = refs
    j = pl.program_id(1)

    @pl.when(j == 0)
    def _():
        x = x_ref[...]
        ms = jnp.mean(x * x, axis=-1, keepdims=True)
        h_ref[...] = (x * lax.rsqrt(ms + NORM_EPS) * nw_ref[...]).astype(BF16)
        o_ref[...] = x

    h = h_ref[...]
    gt = jnp.dot(h, wg_ref[...], preferred_element_type=F32)
    up = jnp.dot(h, wu_ref[...], preferred_element_type=F32)
    act = (_silu(gt) * up).astype(BF16)
    o_ref[...] += jnp.dot(act, wd_ref[...], preferred_element_type=F32)

    if has_final:
        @pl.when(j == pl.num_programs(1) - 1)
        def _():
            y = o_ref[...]
            ms = jnp.mean(y * y, axis=-1, keepdims=True)
            o_ref[...] = y * lax.rsqrt(ms + NORM_EPS) * fw_ref[...]


def _ffn(x, nw, wg, wu, wd, final_w=None, tm=1024, tf=512):
    t, d = x.shape
    f = wg.shape[1]
    has_final = final_w is not None
    in_specs = [pl.BlockSpec((tm, d), lambda i, j: (i, 0)),
                pl.BlockSpec((1, d), lambda i, j: (0, 0)),
                pl.BlockSpec((d, tf), lambda i, j: (0, j)),
                pl.BlockSpec((d, tf), lambda i, j: (0, j)),
                pl.BlockSpec((tf, d), lambda i, j: (j, 0))]
    args = [x, nw.reshape(1, d), wg.astype(BF16), wu.astype(BF16), wd.astype(BF16)]
    if has_final:
        in_specs.append(pl.BlockSpec((1, d), lambda i, j: (0, 0)))
        args.append(final_w.reshape(1, d))
    return pl.pallas_call(
        functools.partial(_ffn_kernel, has_final),
        out_shape=jax.ShapeDtypeStruct((t, d), F32),
        grid=(t // tm, f // tf),
        in_specs=in_specs,
        out_specs=pl.BlockSpec((tm, d), lambda i, j: (i, 0)),
        scratch_shapes=[pltpu.VMEM((tm, d), BF16)],
        compiler_params=_cparams(("parallel", "arbitrary")),
        name="ffn",
    )(*args)


def _pack_kernel(w_ref, o_ref):
    rb, n_in = w_ref.shape
    o_z = RET_COLS
    o_xbc = o_z + SSM_WIDTH
    o_dt = o_xbc + SSM_CONV_DIM
    o_rkv = RET_COLS + SSM_COLS
    o_xw = o_rkv + 3 * RWKV_WIDTH
    o_xa = o_xw + RWKV_DECAY_RANK
    o_xg = o_xa + RWKV_AAA_RANK
    o_pv = IN_COLS
    o_ref[:, :o_dt] = w_ref[:, :o_dt].astype(BF16)
    moves = [(C_RKV, o_rkv, o_xw), (C_XG, o_xg, o_pv), (C_DT, o_dt, o_rkv), (C_XW, o_xw, o_xa), (C_XA, o_xa, o_xg),
             (C_PV, o_pv, n_in)]
    for dst, lo, hi in moves:
        n = hi - lo
        width = -(-max(n, 1) // LANE) * LANE
        if n > 0:
            o_ref[:, dst:dst + n] = w_ref[:, lo:hi].astype(BF16)
        if width > n:
            o_ref[:, dst + n:dst + width] = jnp.zeros((rb, width - n), BF16)


def _pack_w_in(w_in, rb=256):
    d, n_in = w_in.shape
    return pl.pallas_call(
        _pack_kernel,
        out_shape=jax.ShapeDtypeStruct((d, IN_PAD), BF16),
        grid=(d // rb,),
        in_specs=[pl.BlockSpec((rb, n_in), lambda i: (i, 0))],
        out_specs=pl.BlockSpec((rb, IN_PAD), lambda i: (i, 0)),
        compiler_params=_cparams(("parallel",)),
        name="pack_w_in",
    )(w_in)


def kernel(x, norm_mix_w, w_in_first, w_in_rest, ssm_conv_w, ssm_conv_b, ssm_dt_bias, ssm_a_log, ssm_d, ssm_norm_w, rwkv_mu, rwkv_mu_v, rwkv_w0, rwkv_w2, rwkv_a0, rwkv_a2, rwkv_v0, rwkv_v2, rwkv_g2, rwkv_k_k, rwkv_k_a, rwkv_r_k, rwkv_ln_w, rwkv_ln_b, w_out, norm_ffn_w, ffn_w_gate, ffn_w_up, ffn_w_down, final_norm_w):
    b, s, d = x.shape
    depth = norm_mix_w.shape[0]
    xf = x.reshape(b * s, d)
    cos, sin = _rope_tables(s)
    v_first = None
    for l in range(depth):
        w_in = w_in_first if l == 0 else w_in_rest[l - 1]
        proj = _norm_matmul(xf, norm_mix_w[l], _pack_w_in(w_in))
        vres = None if l == 0 else (rwkv_mu_v[l - 1], rwkv_v0[l - 1], rwkv_v2[l - 1], v_first)
        ssm = (ssm_conv_w[l], ssm_conv_b[l], ssm_dt_bias[l], ssm_a_log[l], ssm_d[l], ssm_norm_w[l])
        rw = (rwkv_mu[l], rwkv_w0[l], rwkv_w2[l], rwkv_a0[l], rwkv_a2[l], rwkv_g2[l], rwkv_k_k[l], rwkv_k_a[l],
              rwkv_r_k[l], rwkv_ln_w[l], rwkv_ln_b[l])
        out = _mixers(proj, cos, sin, ssm, rw, vres, b, s)
        y_ret, y_ssm, y_rwkv = out[:3]
        if l == 0:
            v_first = out[3]
        xf = _out_proj(xf, y_ret, y_ssm, y_rwkv, w_out[l])
        xf = _ffn(xf, norm_ffn_w[l], ffn_w_gate[l], ffn_w_up[l], ffn_w_down[l],
                  final_norm_w if l == depth - 1 else None)
    return xf.reshape(b, s, d)
```

```python
import functools
import math

import jax
import jax.numpy as jnp
from jax import lax
from jax.experimental import pallas as pl
from jax.experimental.pallas import tpu as pltpu

F32 = jnp.float32
BF16 = jnp.bfloat16

D_MODEL = 2048
NORM_EPS = 1e-6
RET_HEADS, RET_HEAD_DIM = 4, 128
RET_WIDTH = RET_HEADS * RET_HEAD_DIM
RET_GN_EPS = 1e-6
ROPE_BASE = 10000.0
RET_CHUNK = 128
SSM_HEADS, SSM_HEAD_DIM = 16, 64
SSM_WIDTH = SSM_HEADS * SSM_HEAD_DIM
SSM_GROUPS, SSM_STATE, SSM_CONV = 2, 128, 4
SSM_CONV_DIM = SSM_WIDTH + 2 * SSM_GROUPS * SSM_STATE
SSM_NORM_EPS = 1e-5
SSM_CHUNK = 128
RWKV_HEADS, RWKV_HEAD_DIM = 8, 64
RWKV_WIDTH = RWKV_HEADS * RWKV_HEAD_DIM
RWKV_DECAY_RANK, RWKV_AAA_RANK, RWKV_MV_RANK, RWKV_GATE_RANK = 96, 96, 64, 256
RWKV_LN_EPS = 64e-5
RWKV_CHUNK = 64
RWKV_ROWS = 128
RET_COLS = 4 * RET_WIDTH
SSM_COLS = SSM_WIDTH + SSM_CONV_DIM + SSM_HEADS
RWKV_COLS = 3 * RWKV_WIDTH + RWKV_DECAY_RANK + RWKV_AAA_RANK + RWKV_GATE_RANK
IN_COLS = RET_COLS + SSM_COLS + RWKV_COLS

LANE = 128
VMEM_LIMIT = 56 * 1024 * 1024

C_RET = 0
C_Z = 2048
C_XBC = 3072
C_RKV = 4608
C_XG = 6144
C_DT = 6400
C_XW = 6528
C_XA = 6656
C_PV = 6784
IN_PAD = 6912


def _cparams(sem):
    return pltpu.CompilerParams(dimension_semantics=sem, vmem_limit_bytes=VMEM_LIMIT)


def _split(x, n):
    parts = []
    rem = x
    for i in range(n):
        p = rem.astype(BF16)
        parts.append(p)
        if i + 1 < n:
            rem = rem - p.astype(F32)
    return parts


def _mm(a, b, dims):
    return lax.dot_general(a, b, (dims, ((), ())), preferred_element_type=F32)


def _dot(a, b, pa=1, pb=1, dims=((1,), (0,))):
    ap = _split(a, pa) if a.dtype != BF16 else [a]
    bp = _split(b, pb) if b.dtype != BF16 else [b]
    order = max(len(ap), len(bp))
    acc = None
    for i, x in enumerate(ap):
        for j, y in enumerate(bp):
            if i + j < order:
                t = _mm(x, y, dims)
                acc = t if acc is None else acc + t
    return acc


LOG2E = math.log2(math.e)
NT = ((1,), (1,))
TN = ((0,), (0,))


def _sigmoid(x):
    return 1.0 / (1.0 + jnp.exp(-x))


def _silu(x):
    return x * _sigmoid(x)


def _softplus(x):
    return jnp.maximum(x, 0.0) + jnp.log(1.0 + jnp.exp(-jnp.abs(x)))


def _iota(shape, dim):
    return lax.broadcasted_iota(jnp.int32, shape, dim)


def _shift_rows(x, s, tail):
    xs = pltpu.roll(x, s, axis=0)
    tl = pltpu.roll(tail, s, axis=0)
    head = jnp.where(_iota(tl.shape, 0) < s, tl, xs[:8])
    return jnp.concatenate([head, xs[8:]], axis=0)


def _norm_matmul_kernel(x_ref, nw_ref, w_ref, o_ref, h_ref):
    @pl.when(pl.program_id(1) == 0)
    def _():
        x = x_ref[...]
        ms = jnp.mean(x * x, axis=-1, keepdims=True)
        h_ref[...] = (x * lax.rsqrt(ms + NORM_EPS) * nw_ref[...]).astype(BF16)

    o_ref[...] = jnp.dot(h_ref[...], w_ref[...], preferred_element_type=F32)


def _norm_matmul(x, nw, w, tm=1024, tn=768):
    t, d = x.shape
    n = w.shape[1]
    return pl.pallas_call(
        _norm_matmul_kernel,
        out_shape=jax.ShapeDtypeStruct((t, n), F32),
        grid=(t // tm, n // tn),
        in_specs=[pl.BlockSpec((tm, d), lambda i, j: (i, 0)),
                  pl.BlockSpec((1, d), lambda i, j: (0, 0)),
                  pl.BlockSpec((d, tn), lambda i, j: (0, j))],
        out_specs=pl.BlockSpec((tm, tn), lambda i, j: (i, j)),
        scratch_shapes=[pltpu.VMEM((tm, d), BF16)],
        compiler_params=_cparams(("parallel", "arbitrary")),
        name="norm_in_proj",
    )(x, nw.reshape(1, d), w)


def _rope_kernel(cos_ref, sin_ref):
    rows = cos_ref.shape[0]
    half = RET_HEAD_DIM // 2
    pos = (pl.program_id(0) * rows + _iota((rows, LANE), 0)).astype(F32)
    lane = _iota((rows, LANE), 1)
    j = jnp.where(lane >= half, lane - half, lane).astype(F32)
    inv_freq = jnp.exp(j * (-math.log(ROPE_BASE) / half))
    ang = pos * inv_freq
    cos_ref[...] = jnp.cos(ang)
    sin_ref[...] = jnp.where(lane >= half, jnp.sin(ang), -jnp.sin(ang))


def _rope_tables(s, rows=512):
    return pl.pallas_call(
        _rope_kernel,
        out_shape=(jax.ShapeDtypeStruct((s, LANE), F32), jax.ShapeDtypeStruct((s, LANE), F32)),
        grid=(s // rows,),
        out_specs=(pl.BlockSpec((rows, LANE), lambda i: (i, 0)),
                   pl.BlockSpec((rows, LANE), lambda i: (i, 0))),
        compiler_params=_cparams(("parallel",)),
        name="rope_tables",
    )()


def _retention_body(q_ref, k_ref, v_ref, g_ref, cos_ref, sin_ref, o_ref, st_ref):
    c, dh = RET_CHUNK, RET_HEAD_DIM

    @pl.when(pl.program_id(1) == 0)
    def _():
        st_ref[...] = jnp.zeros_like(st_ref)

    yield
    cos = cos_ref[...]
    sin = sin_ref[...]
    ri = _iota((c, c), 0)
    ci = _iota((c, c), 1)
    rel = (ri - ci).astype(F32)
    causal = ri >= ci
    tcol = _iota((c, 1), 0).astype(F32)
    for h in range(RET_HEADS):
        lg = math.log1p(-(2.0 ** (-5 - h)))
        sl = slice(h * dh, (h + 1) * dh)
        q = q_ref[:, sl]
        k = k_ref[:, sl]
        v = v_ref[:, sl]
        qr = q * cos + pltpu.roll(q, dh // 2, axis=1) * sin
        kr = (k * cos + pltpu.roll(k, dh // 2, axis=1) * sin) * (dh ** -0.5)
        decay = jnp.where(causal, jnp.exp2(jnp.maximum(rel, 0.0) * (lg * LOG2E)), 0.0)
        vb = v.astype(BF16)
        scores = _dot(qr, kr, dims=NT) * decay
        yield
        inner = _dot(scores, vb)
        prev = st_ref[h]
        cross = _dot(qr * jnp.exp((tcol + 1.0) * lg), prev)
        kz = kr * jnp.exp((c - 1.0 - tcol) * lg)
        st_ref[h] = prev * math.exp(c * lg) + _dot(kz.T, vb)
        yield
        y = inner + cross
        yc = y - jnp.mean(y, axis=-1, keepdims=True)
        yn = yc * lax.rsqrt(jnp.mean(yc * yc, axis=-1, keepdims=True) + RET_GN_EPS)
        o_ref[:, sl] = (yn * _silu(g_ref[:, sl])).astype(BF16)
        yield


def _ssd_body(z_ref, xbc_ref, dt_ref, cw_ref, cb_ref, dtb_ref, alog_ref, dsk_ref, nw_ref, ex_ref,
              o_ref, st_ref, tail_ref):
    c, p2, nst = SSM_CHUNK, 2 * SSM_HEAD_DIM, SSM_STATE
    gn = SSM_GROUPS * SSM_STATE

    @pl.when(pl.program_id(1) == 0)
    def _():
        st_ref[...] = jnp.zeros_like(st_ref)
        tail_ref[...] = jnp.zeros_like(tail_ref)

    yield
    raw = xbc_ref[...]
    tail = tail_ref[...]
    cw = cw_ref[...]
    conv = raw * cw[3:4] + cb_ref[...]
    for s in range(1, SSM_CONV):
        conv = conv + _shift_rows(raw, s, tail) * cw[3 - s:4 - s]
    tail_ref[...] = raw[c - 8:]
    yield
    xbc = _silu(conv)
    xs = xbc[:, :SSM_WIDTH]
    bm = xbc[:, SSM_WIDTH:SSM_WIDTH + gn]
    cm = xbc[:, SSM_WIDTH + gn:]
    yield

    dt = _softplus(dt_ref[...] + dtb_ref[...])
    a = dt * (-LOG2E * jnp.exp(alog_ref[...]))
    ri = _iota((c, c), 0)
    ci = _iota((c, c), 1)
    causal = ri >= ci
    tri = jnp.where(causal, 1.0, 0.0).astype(BF16)
    acum = _dot(tri, a, pb=3)
    acum_t = acum.T
    ex = ex_ref[...]
    dt_e = _dot(dt, ex, pa=3)
    acum_e = _dot(acum, ex, pa=3)
    a_last = acum_e[c - 1:c]
    from_start = jnp.exp2(acum_e)
    to_end = jnp.exp2(a_last - acum_e)
    chunk_decay = jnp.exp2(a_last)
    x_dt = xs * dt_e
    lane = _iota((1, p2), 1)
    mask_l = jnp.where(lane < SSM_HEAD_DIM, 1.0, 0.0)
    mask_r = 1.0 - mask_l
    yield

    pairs_per_group = SSM_HEADS // SSM_GROUPS // 2
    ys = []
    for g in range(SSM_GROUPS):
        bg = bm[:, g * nst:(g + 1) * nst]
        cg = cm[:, g * nst:(g + 1) * nst]
        cbm = _dot(cg, bg, dims=NT)
        bgt = bg.T.astype(BF16)
        cgb = cg.astype(BF16)
        for pp in range(pairs_per_group):
            p = g * pairs_per_group + pp
            sl = slice(p * p2, (p + 1) * p2)
            xp = x_dt[:, sl]
            y = None
            for hh, msk in ((2 * p, mask_l), (2 * p + 1, mask_r)):
                seg = acum[:, hh:hh + 1] - acum_t[hh:hh + 1, :]
                m = cbm * jnp.where(causal, jnp.exp2(jnp.minimum(seg, 0.0)), 0.0)
                t = _dot(m, xp * msk)
                y = t if y is None else y + t
            prev = st_ref[p]
            y = y + _dot(cgb, prev) * from_start[:, sl]
            st_ref[p] = prev * chunk_decay[:, sl] + _dot(bgt, xp * to_end[:, sl])
            ys.append(y + xs[:, sl] * dsk_ref[:, sl])
            yield
    y = jnp.concatenate(ys, axis=1) * _silu(z_ref[...])
    gw = SSM_WIDTH // SSM_GROUPS
    nw = nw_ref[...]
    for g in range(SSM_GROUPS):
        sl = slice(g * gw, (g + 1) * gw)
        yg = y[:, sl]
        ms = jnp.mean(yg * yg, axis=-1, keepdims=True)
        o_ref[:, sl] = (yg * lax.rsqrt(ms + SSM_NORM_EPS) * nw[:, sl]).astype(BF16)
        yield


def _rwkv_body(has_vres, *refs):
    if has_vres:
        (r_ref, k_ref, v_ref, xg_ref, xw_ref, xa_ref, pv_ref, vf_ref,
         mu_rkv_ref, mu_g_ref, mu_w_ref, mu_a_ref, mu_v_ref,
         w0_ref, w2_ref, a0_ref, a2_ref, v0_ref, v2_ref, g2_ref, kk_ref, ka_ref, rk_ref,
         lnw_ref, lnb_ref, o_ref, st_ref, tail_ref) = refs
    else:
        (r_ref, k_ref, v_ref, xg_ref, xw_ref, xa_ref,
         mu_rkv_ref, mu_g_ref, mu_w_ref, mu_a_ref,
         w0_ref, w2_ref, a0_ref, a2_ref, g2_ref, kk_ref, ka_ref, rk_ref,
         lnw_ref, lnb_ref, o_ref, vo_ref, st_ref, tail_ref) = refs
    c, w, p2 = RWKV_CHUNK, RWKV_WIDTH, 2 * RWKV_HEAD_DIM
    rows = o_ref.shape[0]
    nsub = rows // c

    @pl.when(pl.program_id(1) == 0)
    def _():
        st_ref[...] = jnp.zeros_like(st_ref)
        tail_ref[...] = jnp.zeros_like(tail_ref)

    yield
    def mixed(ref, mu, off):
        x = ref[...]
        wd = x.shape[1]
        prev = _shift_rows(x, 1, tail_ref[:, off:off + wd])
        tail_ref[:, off:off + wd] = x[rows - 8:]
        return x + (prev - x) * mu

    mu_rkv = mu_rkv_ref[...]
    r = mixed(r_ref, mu_rkv[:, :w], 0)
    k = mixed(k_ref, mu_rkv[:, w:2 * w], w)
    v = mixed(v_ref, mu_rkv[:, 2 * w:], 2 * w)
    xg = mixed(xg_ref, mu_g_ref[...], 3 * w)
    xw = mixed(xw_ref, mu_w_ref[...], 3 * w + 256)
    xa = mixed(xa_ref, mu_a_ref[...], 3 * w + 384)
    yield

    w_log = -_softplus(-(w0_ref[...] + _dot(jnp.tanh(xw), w2_ref[...], pa=3, pb=3))) - 0.5
    lw = -LOG2E * jnp.exp(w_log)
    gate = _sigmoid(a0_ref[...] + _dot(xa, a2_ref[...]))
    g = _dot(_sigmoid(xg), g2_ref[...])
    yield
    if has_vres:
        pv = mixed(pv_ref, mu_v_ref[...], 3 * w + 512)
        v = v + (vf_ref[...] - v) * _sigmoid(v0_ref[...] + _dot(pv, v2_ref[...]))
    else:
        vo_ref[...] = v

    lane = _iota((1, p2), 1)
    mask_l = jnp.where(lane < RWKV_HEAD_DIM, 1.0, 0.0)
    mask_r = 1.0 - mask_l
    r2 = _iota((p2, p2), 0)
    c2 = _iota((p2, p2), 1)
    same = (r2 < RWKV_HEAD_DIM) == (c2 < RWKV_HEAD_DIM)
    bd_ones = jnp.where(same, 1.0, 0.0).astype(BF16)

    def head_sum(x, pa):
        return jnp.concatenate(
            [_dot(x[:, i * p2:(i + 1) * p2], bd_ones, pa=pa) for i in range(w // p2)], axis=1)

    kk = k * kk_ref[...]
    kk = kk * jnp.minimum(lax.rsqrt(head_sum(kk * kk, 1)), 1e12)
    k = k * (1.0 + (gate - 1.0) * ka_ref[...])
    kb = kk * gate
    yield

    ri = _iota((rows, rows), 0)
    ci = _iota((rows, rows), 1)
    in_chunk = (ri >= ci) & ((ri - ci) <= (ri & (c - 1)))
    tri = jnp.where(in_chunk, 1.0, 0.0).astype(BF16)
    cum = _dot(tri, lw, pb=2)
    c_ends = [cum[(j + 1) * c - 1:(j + 1) * c] for j in range(nsub)]
    c_last = jnp.concatenate([jnp.broadcast_to(x, (c, w)) for x in c_ends], axis=0)
    g_t = jnp.exp2(cum)
    g_inv = jnp.exp2(-cum)
    g_end = jnp.exp2(c_last - cum)
    g_all = [jnp.exp2(x) for x in c_ends]
    a_s = -kk * jnp.exp2(cum - lw)
    r_s = r * g_t
    b_h = kb * g_inv
    k_h = k * g_inv
    b_e = kb * g_end
    k_e = k * g_end
    yield

    lower = (r2 >= c2) & same
    strict = (r2 > c2) & same
    eye = jnp.where(r2 == c2, 1.0, 0.0)

    def stack(x):
        return jnp.concatenate([x * mask_l, x * mask_r], axis=0)

    def twice(x):
        return jnp.concatenate([x, x], axis=0)

    def bf(x):
        return x.astype(BF16)

    def mm(a, b):
        return _mm(a, b, ((1,), (0,)))

    npair = w // p2
    pairs = range(nsub * npair)
    sls = [(slice((q // npair) * c, (q // npair + 1) * c), slice((q % npair) * p2, (q % npair + 1) * p2))
           for q in pairs]
    sa = [bf(stack(a_s[s])) for s in sls]
    sr = [stack(r_s[s]) for s in sls]
    sv = [bf(stack(v[s])) for s in sls]
    yield
    gram = [_mm(jnp.concatenate([sa[p], bf(sr[p])], axis=0),
                bf(jnp.concatenate([twice(b_h[sls[p]]), twice(k_h[sls[p]])], axis=0)), NT)
            for p in pairs]
    n = [jnp.where(strict, x[:p2, :p2], 0.0) for x in gram]
    a_ak = [bf(jnp.where(strict, x[:p2, p2:], 0.0)) for x in gram]
    a_rb = [bf(jnp.where(lower, x[p2:, :p2], 0.0)) for x in gram]
    a_rk = [bf(jnp.where(lower, x[p2:, p2:], 0.0)) for x in gram]
    yield
    t = [eye + x for x in n]
    pw = [bf(x) for x in n]
    for _ in range(5):
        pw = [bf(mm(x, x)) for x in pw]
        t = [t[p] + mm(bf(t[p]), pw[p]) for p in pairs]
        yield
    av = [bf(mm(a_ak[p], sv[p])) for p in pairs]
    yield
    au = [bf(mm(bf(t[p]), jnp.concatenate([sa[p], av[p]], axis=1))) for p in pairs]
    yield
    zero = jnp.zeros((p2, p2), BF16)
    z = [mm(jnp.concatenate([jnp.concatenate([a_rb[p], a_rk[p]], axis=1),
                             jnp.concatenate([bf(stack(b_e[sls[p]]).T), bf(stack(k_e[sls[p]]).T)], axis=1)],
                            axis=0),
            jnp.concatenate([au[p], jnp.concatenate([zero, sv[p]], axis=1)], axis=0))
         for p in pairs]
    lhs = [bf(jnp.concatenate([sr[p] + z[p][:p2, :p2], z[p][p2:, :p2]], axis=0)) for p in pairs]
    yield
    g_col = [jnp.sum(eye * g_all[q // npair][:, sls[q][1]], axis=1, keepdims=True) for q in pairs]
    ys = [[None] * npair for _ in range(nsub)]
    for p in range(npair):
        h = st_ref[p]
        for j in range(nsub):
            q = j * npair + p
            hi, lo = _split(h, 2)
            out = mm(jnp.concatenate([lhs[q], lhs[q]], axis=1), jnp.concatenate([hi, lo], axis=0)) + z[q][:, p2:]
            h = out[p2:] + g_col[q] * h
            ys[j][p] = out[:c] + out[c:p2]
        st_ref[p] = h
    yield
    y = jnp.concatenate([jnp.concatenate(row, axis=1) for row in ys], axis=0)

    mean = head_sum(y, 1) * (1.0 / RWKV_HEAD_DIM)
    yc = y - mean
    var = head_sum(yc * yc, 1) * (1.0 / RWKV_HEAD_DIM)
    yn = yc * lax.rsqrt(var + RWKV_LN_EPS) * lnw_ref[...] + lnb_ref[...]
    bonus = head_sum(r * k * rk_ref[...], 1) * v
    o_ref[...] = ((yn + bonus) * g).astype(BF16)
    yield


def _interleave(gens, lead):
    for g in gens:
        next(g)
    for _ in range(lead):
        next(gens[0])
    live = list(gens)
    while live:
        for g in list(live):
            try:
                next(g)
            except StopIteration:
                live.remove(g)


N_RET_IN, N_SSD_IN = 6, 10


def _mixers_kernel(has_vres, n_rw_in, *refs):
    ret_in = refs[:N_RET_IN]
    ssd_in = refs[N_RET_IN:N_RET_IN + N_SSD_IN]
    rw_in = refs[N_RET_IN + N_SSD_IN:N_RET_IN + N_SSD_IN + n_rw_in]
    rest = refs[N_RET_IN + N_SSD_IN + n_rw_in:]
    n_out = 3 if has_vres else 4
    y_ret, y_ssm = rest[0], rest[1]
    rw_out = rest[2:n_out]
    ret_st, ssd_st, ssd_tail, rw_st, rw_tail = rest[n_out:]
    _interleave([_rwkv_body(has_vres, *rw_in, *rw_out, rw_st, rw_tail),
                 _ssd_body(*ssd_in, y_ssm, ssd_st, ssd_tail),
                 _retention_body(*ret_in, y_ret, ret_st)], lead=5)


def _mixer_params(ssm_conv_w, ssm_conv_b, ssm_dt_bias, ssm_a_log, ssm_d, ssm_norm_w, rwkv_mu, rwkv_mu_v, rwkv_w0,
                  rwkv_w2, rwkv_a0, rwkv_a2, rwkv_v0, rwkv_v2, rwkv_g2, rwkv_k_k, rwkv_k_a, rwkv_r_k, rwkv_ln_w,
                  rwkv_ln_b):
    w = RWKV_WIDTH
    o1 = 3 * w + RWKV_DECAY_RANK
    o2 = o1 + RWKV_AAA_RANK

    def vec(x, n=None):
        n = x.shape[-1] if n is None else n
        return jnp.pad(x, ((0, 0), (0, n - x.shape[-1])))[:, None, :]

    def mat(x, r):
        return jnp.pad(x, ((0, 0), (0, r - x.shape[1]), (0, 0))).astype(BF16)

    ssd = [jnp.pad(ssm_conv_w, ((0, 0), (0, 8 - SSM_CONV), (0, 0))), vec(ssm_conv_b), vec(ssm_dt_bias, LANE),
           vec(ssm_a_log, LANE), vec(jnp.repeat(ssm_d, SSM_HEAD_DIM, axis=1)), vec(ssm_norm_w)]
    mus = [vec(rwkv_mu[:, :3 * w]), vec(rwkv_mu[:, o2:]), vec(rwkv_mu[:, 3 * w:o1], LANE),
           vec(rwkv_mu[:, o1:o2], LANE)]
    low_rank = [vec(rwkv_w0), mat(rwkv_w2, LANE), vec(rwkv_a0), mat(rwkv_a2, LANE)]
    rest = [rwkv_g2.astype(BF16), vec(rwkv_k_k), vec(rwkv_k_a), vec(rwkv_r_k.reshape(-1, w)), vec(rwkv_ln_w),
            vec(rwkv_ln_b)]
    vres = [vec(rwkv_mu_v, LANE), vec(rwkv_v0), mat(rwkv_v2, LANE)]
    expand = (jnp.arange(LANE)[:, None] == (jnp.arange(SSM_WIDTH)[None, :] // SSM_HEAD_DIM)).astype(BF16)
    return dict(ssd=ssd, mus=mus, low_rank=low_rank, rest=rest, vres=vres, expand=expand)


def _mixers(proj, cos, sin, prm, layer, v_first, b, s):
    rows = RWKV_ROWS
    assert rows == RET_CHUNK == SSM_CHUNK
    nc = s // rows
    w = RWKV_WIDTH
    has_vres = v_first is not None

    def row(wd, col):
        return pl.BlockSpec((rows, wd), lambda bi, n, j=col // wd: (bi * nc + n, j))

    def lay(x, idx):
        return pl.BlockSpec((None,) + x.shape[1:], lambda bi, n: (idx, 0, 0))

    tab = pl.BlockSpec((rows, LANE), lambda bi, n: (n, 0))
    ret_args = [proj, proj, proj, proj, cos, sin]
    ret_specs = [row(RET_WIDTH, C_RET + i * RET_WIDTH) for i in range(4)] + [tab, tab]

    ssd_args = [proj, proj, proj] + prm["ssd"] + [prm["expand"]]
    ssd_specs = ([row(SSM_WIDTH, C_Z), row(SSM_CONV_DIM, C_XBC), row(LANE, C_DT)]
                 + [lay(x, layer) for x in prm["ssd"]] + [pl.BlockSpec(prm["expand"].shape, lambda bi, n: (0, 0))])
    assert len(ret_args) == N_RET_IN and len(ssd_args) == N_SSD_IN

    rw_acts = [proj] * 6
    rw_act_specs = [row(w, C_RKV), row(w, C_RKV + w), row(w, C_RKV + 2 * w), row(256, C_XG),
                    row(LANE, C_XW), row(LANE, C_XA)]
    mus = [(x, layer) for x in prm["mus"]]
    par = [(x, layer) for x in prm["low_rank"]]
    if has_vres:
        mu_v, v0, v2 = prm["vres"]
        rw_acts += [proj, v_first]
        rw_act_specs += [row(LANE, C_PV), pl.BlockSpec((rows, w), lambda bi, n: (bi * nc + n, 0))]
        mus.append((mu_v, layer - 1))
        par += [(v0, layer - 1), (v2, layer - 1)]
    par += [(x, layer) for x in prm["rest"]]
    rw_consts = mus + par
    rw_args = rw_acts + [x for x, _ in rw_consts]
    rw_specs = rw_act_specs + [lay(x, idx) for x, idx in rw_consts]

    def out_block(wd):
        return pl.BlockSpec((rows, wd), lambda bi, n: (bi * nc + n, 0))

    t = b * s
    out_shape = [jax.ShapeDtypeStruct((t, RET_WIDTH), BF16), jax.ShapeDtypeStruct((t, SSM_WIDTH), BF16),
                 jax.ShapeDtypeStruct((t, w), BF16)]
    out_specs = [out_block(RET_WIDTH), out_block(SSM_WIDTH), out_block(w)]
    if not has_vres:
        out_shape.append(jax.ShapeDtypeStruct((t, w), F32))
        out_specs.append(out_block(w))
    return pl.pallas_call(
        functools.partial(_mixers_kernel, has_vres, len(rw_args)),
        out_shape=tuple(out_shape),
        grid=(b, nc),
        in_specs=ret_specs + ssd_specs + rw_specs,
        out_specs=tuple(out_specs),
        scratch_shapes=[pltpu.VMEM((RET_HEADS, RET_HEAD_DIM, RET_HEAD_DIM), F32),
                        pltpu.VMEM((SSM_HEADS // 2, SSM_STATE, 2 * SSM_HEAD_DIM), F32),
                        pltpu.VMEM((8, SSM_CONV_DIM), F32),
                        pltpu.VMEM((RWKV_HEADS // 2, 2 * RWKV_HEAD_DIM, 2 * RWKV_HEAD_DIM), F32),
                        pltpu.VMEM((8, 3 * w + 256 + 3 * LANE), F32)],
        compiler_params=_cparams(("parallel", "arbitrary")),
        name="mixers",
    )(*ret_args, *ssd_args, *rw_args)


def _out_proj_kernel(x_ref, yr_ref, ys_ref, yw_ref, w_ref, o_ref):
    r0, r1 = RET_WIDTH, RET_WIDTH + SSM_WIDTH
    acc = x_ref[...]
    acc = acc + jnp.dot(yr_ref[...], w_ref[:r0], preferred_element_type=F32)
    acc = acc + jnp.dot(ys_ref[...], w_ref[r0:r1], preferred_element_type=F32)
    acc = acc + jnp.dot(yw_ref[...], w_ref[r1:], preferred_element_type=F32)
    o_ref[...] = acc


def _out_proj(x, y_ret, y_ssm, y_rwkv, w_out_all, layer, tm=512):
    t, d = x.shape

    def rows(wd):
        return pl.BlockSpec((tm, wd), lambda i: (i, 0))

    return pl.pallas_call(
        _out_proj_kernel,
        out_shape=jax.ShapeDtypeStruct((t, d), F32),
        grid=(t // tm,),
        in_specs=[rows(d), rows(RET_WIDTH), rows(SSM_WIDTH), rows(RWKV_WIDTH),
                  pl.BlockSpec((None,) + w_out_all.shape[1:], lambda i: (layer, 0, 0))],
        out_specs=rows(d),
        compiler_params=_cparams(("parallel",)),
        name="out_proj",
    )(x, y_ret, y_ssm, y_rwkv, w_out_all)


def _ffn_kernel(has_final, *refs):
    if has_final:
        x_ref, nw_ref, wg_ref, wu_ref, wd_ref, fw_ref, o_ref, h_ref = refs
    else:
        x_ref, nw_ref, wg_ref, wu_ref, wd_ref, o_ref, h_ref = refs
    j = pl.program_id(1)

    @pl.when(j == 0)
    def _():
        x = x_ref[...]
        ms = jnp.mean(x * x, axis=-1, keepdims=True)
        h_ref[...] = (x * lax.rsqrt(ms + NORM_EPS) * nw_ref[...]).astype(BF16)
        o_ref[...] = x

    h = h_ref[...]
    gt = jnp.dot(h, wg_ref[...], preferred_element_type=F32)
    up = jnp.dot(h, wu_ref[...], preferred_element_type=F32)
    act = (_silu(gt) * up).astype(BF16)
    o_ref[...] += jnp.dot(act, wd_ref[...], preferred_element_type=F32)

    if has_final:
        @pl.when(j == pl.num_programs(1) - 1)
        def _():
            y = o_ref[...]
            ms = jnp.mean(y * y, axis=-1, keepdims=True)
            o_ref[...] = y * lax.rsqrt(ms + NORM_EPS) * fw_ref[...]


def _ffn(x, nw, wg_all, wu_all, wd_all, layer, final_w=None, tm=1024, tf=512):
    t, d = x.shape
    f = wg_all.shape[2]
    has_final = final_w is not None
    in_specs = [pl.BlockSpec((tm, d), lambda i, j: (i, 0)),
                pl.BlockSpec((1, d), lambda i, j: (0, 0)),
                pl.BlockSpec((None, d, tf), lambda i, j: (layer, 0, j)),
                pl.BlockSpec((None, d, tf), lambda i, j: (layer, 0, j)),
                pl.BlockSpec((None, tf, d), lambda i, j: (layer, j, 0))]
    args = [x, nw.reshape(1, d), wg_all, wu_all, wd_all]
    if has_final:
        in_specs.append(pl.BlockSpec((1, d), lambda i, j: (0, 0)))
        args.append(final_w.reshape(1, d))
    return pl.pallas_call(
        functools.partial(_ffn_kernel, has_final),
        out_shape=jax.ShapeDtypeStruct((t, d), F32),
        grid=(t // tm, f // tf),
        in_specs=in_specs,
        out_specs=pl.BlockSpec((tm, d), lambda i, j: (i, 0)),
        scratch_shapes=[pltpu.VMEM((tm, d), BF16)],
        compiler_params=_cparams(("parallel", "arbitrary")),
        name="ffn",
    )(*args)


def _pack_kernel(w_ref, o_ref):
    rb, n_in = w_ref.shape
    o_z = RET_COLS
    o_xbc = o_z + SSM_WIDTH
    o_dt = o_xbc + SSM_CONV_DIM
    o_rkv = RET_COLS + SSM_COLS
    o_xw = o_rkv + 3 * RWKV_WIDTH
    o_xa = o_xw + RWKV_DECAY_RANK
    o_xg = o_xa + RWKV_AAA_RANK
    o_pv = IN_COLS
    o_ref[:, :o_dt] = w_ref[:, :o_dt].astype(BF16)
    moves = [(C_RKV, o_rkv, o_xw), (C_XG, o_xg, o_pv), (C_DT, o_dt, o_rkv), (C_XW, o_xw, o_xa), (C_XA, o_xa, o_xg),
             (C_PV, o_pv, n_in)]
    for dst, lo, hi in moves:
        n = hi - lo
        width = -(-max(n, 1) // LANE) * LANE
        if n > 0:
            o_ref[:, dst:dst + n] = w_ref[:, lo:hi].astype(BF16)
        if width > n:
            o_ref[:, dst + n:dst + width] = jnp.zeros((rb, width - n), BF16)


def _pack_w_in(w_in_all, layer, rb=256):
    _, d, n_in = w_in_all.shape
    return pl.pallas_call(
        _pack_kernel,
        out_shape=jax.ShapeDtypeStruct((d, IN_PAD), BF16),
        grid=(d // rb,),
        in_specs=[pl.BlockSpec((None, rb, n_in), lambda i: (layer, i, 0))],
        out_specs=pl.BlockSpec((rb, IN_PAD), lambda i: (i, 0)),
        compiler_params=_cparams(("parallel",)),
        name="pack_w_in",
    )(w_in_all)


def kernel(x, norm_mix_w, w_in_first, w_in_rest, ssm_conv_w, ssm_conv_b, ssm_dt_bias, ssm_a_log, ssm_d, ssm_norm_w, rwkv_mu, rwkv_mu_v, rwkv_w0, rwkv_w2, rwkv_a0, rwkv_a2, rwkv_v0, rwkv_v2, rwkv_g2, rwkv_k_k, rwkv_k_a, rwkv_r_k, rwkv_ln_w, rwkv_ln_b, w_out, norm_ffn_w, ffn_w_gate, ffn_w_up, ffn_w_down, final_norm_w):
    b, s, d = x.shape
    depth = norm_mix_w.shape[0]
    xf = x.reshape(b * s, d)
    cos, sin = _rope_tables(s)
    wo_all, wg_all, wu_all, wd_all = (w.astype(BF16) for w in (w_out, ffn_w_gate, ffn_w_up, ffn_w_down))
    prm = _mixer_params(ssm_conv_w, ssm_conv_b, ssm_dt_bias, ssm_a_log, ssm_d, ssm_norm_w, rwkv_mu, rwkv_mu_v,
                        rwkv_w0, rwkv_w2, rwkv_a0, rwkv_a2, rwkv_v0, rwkv_v2, rwkv_g2, rwkv_k_k, rwkv_k_a, rwkv_r_k,
                        rwkv_ln_w, rwkv_ln_b)
    v_first = None
    for l in range(depth):
        w_packed = _pack_w_in(w_in_first[None], 0) if l == 0 else _pack_w_in(w_in_rest, l - 1)
        proj = _norm_matmul(xf, norm_mix_w[l], w_packed)
        out = _mixers(proj, cos, sin, prm, l, v_first, b, s)
        y_ret, y_ssm, y_rwkv = out[:3]
        if l == 0:
            v_first = out[3]
        xf = _out_proj(xf, y_ret, y_ssm, y_rwkv, wo_all, l)
        xf = _ffn(xf, norm_ffn_w[l], wg_all, wu_all, wd_all, l, final_norm_w if l == depth - 1 else None)
    return xf.reshape(b, s, d)
```

```python
import functools
import math

import jax
import jax.numpy as jnp
from jax import lax
from jax.experimental import pallas as pl
from jax.experimental.pallas import tpu as pltpu

F32 = jnp.float32
BF16 = jnp.bfloat16

D_MODEL = 2048
NORM_EPS = 1e-6
RET_HEADS, RET_HEAD_DIM = 4, 128
RET_WIDTH = RET_HEADS * RET_HEAD_DIM
RET_GN_EPS = 1e-6
ROPE_BASE = 10000.0
RET_CHUNK = 128
SSM_HEADS, SSM_HEAD_DIM = 16, 64
SSM_WIDTH = SSM_HEADS * SSM_HEAD_DIM
SSM_GROUPS, SSM_STATE, SSM_CONV = 2, 128, 4
SSM_CONV_DIM = SSM_WIDTH + 2 * SSM_GROUPS * SSM_STATE
SSM_NORM_EPS = 1e-5
SSM_CHUNK = 128
RWKV_HEADS, RWKV_HEAD_DIM = 8, 64
RWKV_WIDTH = RWKV_HEADS * RWKV_HEAD_DIM
RWKV_DECAY_RANK, RWKV_AAA_RANK, RWKV_MV_RANK, RWKV_GATE_RANK = 96, 96, 64, 256
RWKV_LN_EPS = 64e-5
RWKV_CHUNK = 64
RWKV_ROWS = 128
RET_COLS = 4 * RET_WIDTH
SSM_COLS = SSM_WIDTH + SSM_CONV_DIM + SSM_HEADS
RWKV_COLS = 3 * RWKV_WIDTH + RWKV_DECAY_RANK + RWKV_AAA_RANK + RWKV_GATE_RANK
IN_COLS = RET_COLS + SSM_COLS + RWKV_COLS

LANE = 128
VMEM_LIMIT = 56 * 1024 * 1024

C_RET = 0
C_Z = 2048
C_XBC = 3072
C_RKV = 4608
C_XG = 6144
C_DT = 6400
C_XW = 6528
C_XA = 6656
C_PV = 6784
IN_PAD = 6912


def _cparams(sem):
    return pltpu.CompilerParams(dimension_semantics=sem, vmem_limit_bytes=VMEM_LIMIT)


def _split(x, n):
    parts = []
    rem = x
    for i in range(n):
        p = rem.astype(BF16)
        parts.append(p)
        if i + 1 < n:
            rem = rem - p.astype(F32)
    return parts


def _mm(a, b, dims):
    return lax.dot_general(a, b, (dims, ((), ())), preferred_element_type=F32)


def _dot(a, b, pa=1, pb=1, dims=((1,), (0,))):
    ap = _split(a, pa) if a.dtype != BF16 else [a]
    bp = _split(b, pb) if b.dtype != BF16 else [b]
    order = max(len(ap), len(bp))
    acc = None
    for i, x in enumerate(ap):
        for j, y in enumerate(bp):
            if i + j < order:
                t = _mm(x, y, dims)
                acc = t if acc is None else acc + t
    return acc


LOG2E = math.log2(math.e)
NT = ((1,), (1,))
TN = ((0,), (0,))


def _sigmoid(x):
    return 1.0 / (1.0 + jnp.exp(-x))


def _silu(x):
    return x * _sigmoid(x)


def _softplus(x):
    return jnp.maximum(x, 0.0) + jnp.log(1.0 + jnp.exp(-jnp.abs(x)))


def _iota(shape, dim):
    return lax.broadcasted_iota(jnp.int32, shape, dim)


def _shift_rows(x, s, tail):
    xs = pltpu.roll(x, s, axis=0)
    tl = pltpu.roll(tail, s, axis=0)
    head = jnp.where(_iota(tl.shape, 0) < s, tl, xs[:8])
    return jnp.concatenate([head, xs[8:]], axis=0)


def _norm_matmul_kernel(x_ref, nw_ref, w_ref, o_ref, h_ref):
    @pl.when(pl.program_id(1) == 0)
    def _():
        x = x_ref[...]
        ms = jnp.mean(x * x, axis=-1, keepdims=True)
        h_ref[...] = (x * lax.rsqrt(ms + NORM_EPS) * nw_ref[...]).astype(BF16)

    o_ref[...] = jnp.dot(h_ref[...], w_ref[...], preferred_element_type=F32).astype(o_ref.dtype)


def _norm_matmul(x, nw, w, tm=1024, tn=2304):
    t, d = x.shape
    n = w.shape[1]
    return pl.pallas_call(
        _norm_matmul_kernel,
        out_shape=jax.ShapeDtypeStruct((t, n), BF16),
        grid=(t // tm, n // tn),
        in_specs=[pl.BlockSpec((tm, d), lambda i, j: (i, 0)),
                  pl.BlockSpec((1, d), lambda i, j: (0, 0)),
                  pl.BlockSpec((d, tn), lambda i, j: (0, j))],
        out_specs=pl.BlockSpec((tm, tn), lambda i, j: (i, j)),
        scratch_shapes=[pltpu.VMEM((tm, d), BF16)],
        compiler_params=_cparams(("parallel", "arbitrary")),
        name="norm_in_proj",
    )(x, nw.reshape(1, d), w)


def _rope_kernel(cos_ref, sin_ref):
    rows = cos_ref.shape[0]
    half = RET_HEAD_DIM // 2
    pos = (pl.program_id(0) * rows + _iota((rows, LANE), 0)).astype(F32)
    lane = _iota((rows, LANE), 1)
    j = jnp.where(lane >= half, lane - half, lane).astype(F32)
    inv_freq = jnp.exp(j * (-math.log(ROPE_BASE) / half))
    ang = pos * inv_freq
    cos_ref[...] = jnp.cos(ang)
    sin_ref[...] = jnp.where(lane >= half, jnp.sin(ang), -jnp.sin(ang))


def _rope_tables(s, rows=512):
    return pl.pallas_call(
        _rope_kernel,
        out_shape=(jax.ShapeDtypeStruct((s, LANE), F32), jax.ShapeDtypeStruct((s, LANE), F32)),
        grid=(s // rows,),
        out_specs=(pl.BlockSpec((rows, LANE), lambda i: (i, 0)),
                   pl.BlockSpec((rows, LANE), lambda i: (i, 0))),
        compiler_params=_cparams(("parallel",)),
        name="rope_tables",
    )()


def _retention_body(q_ref, k_ref, v_ref, g_ref, cos_ref, sin_ref, o_ref, st_ref):
    c, dh = RET_CHUNK, RET_HEAD_DIM

    @pl.when(pl.program_id(1) == 0)
    def _():
        st_ref[...] = jnp.zeros_like(st_ref)

    yield
    cos = cos_ref[...]
    sin = sin_ref[...]
    ri = _iota((c, c), 0)
    ci = _iota((c, c), 1)
    rel = (ri - ci).astype(F32)
    causal = ri >= ci
    tcol = _iota((c, 1), 0).astype(F32)
    for h in range(RET_HEADS):
        lg = math.log1p(-(2.0 ** (-5 - h)))
        sl = slice(h * dh, (h + 1) * dh)
        q = q_ref[:, sl].astype(F32)
        k = k_ref[:, sl].astype(F32)
        v = v_ref[:, sl]
        qr = q * cos + pltpu.roll(q, dh // 2, axis=1) * sin
        kr = (k * cos + pltpu.roll(k, dh // 2, axis=1) * sin) * (dh ** -0.5)
        decay = jnp.where(causal, jnp.exp2(jnp.maximum(rel, 0.0) * (lg * LOG2E)), 0.0)
        vb = v.astype(BF16)
        scores = _dot(qr, kr, dims=NT) * decay
        yield
        inner = _dot(scores, vb)
        prev = st_ref[h]
        cross = _dot(qr * jnp.exp((tcol + 1.0) * lg), prev)
        kz = kr * jnp.exp((c - 1.0 - tcol) * lg)
        st_ref[h] = prev * math.exp(c * lg) + _dot(kz.T, vb)
        yield
        y = inner + cross
        yc = y - jnp.mean(y, axis=-1, keepdims=True)
        yn = yc * lax.rsqrt(jnp.mean(yc * yc, axis=-1, keepdims=True) + RET_GN_EPS)
        o_ref[:, sl] = (yn * _silu(g_ref[:, sl].astype(F32))).astype(BF16)
        yield


def _ssd_body(z_ref, xbc_ref, dt_ref, cw_ref, cb_ref, dtb_ref, alog_ref, dsk_ref, nw_ref, ex_ref,
              o_ref, st_ref, tail_ref):
    c, p2, nst = SSM_CHUNK, 2 * SSM_HEAD_DIM, SSM_STATE
    gn = SSM_GROUPS * SSM_STATE

    @pl.when(pl.program_id(1) == 0)
    def _():
        st_ref[...] = jnp.zeros_like(st_ref)
        tail_ref[...] = jnp.zeros_like(tail_ref)

    yield
    raw = xbc_ref[...].astype(F32)
    tail = tail_ref[...]
    cw = cw_ref[...]
    conv = raw * cw[3:4] + cb_ref[...]
    for s in range(1, SSM_CONV):
        conv = conv + _shift_rows(raw, s, tail) * cw[3 - s:4 - s]
    tail_ref[...] = raw[c - 8:]
    yield
    xbc = _silu(conv)
    xs = xbc[:, :SSM_WIDTH]
    bm = xbc[:, SSM_WIDTH:SSM_WIDTH + gn]
    cm = xbc[:, SSM_WIDTH + gn:]
    yield

    dt = _softplus(dt_ref[...].astype(F32) + dtb_ref[...])
    a = dt * (-LOG2E * jnp.exp(alog_ref[...]))
    ri = _iota((c, c), 0)
    ci = _iota((c, c), 1)
    causal = ri >= ci
    tri = jnp.where(causal, 1.0, 0.0).astype(BF16)
    acum = _dot(tri, a, pb=3)
    acum_t = acum.T
    ex = ex_ref[...]
    dt_e = _dot(dt, ex, pa=3)
    acum_e = _dot(acum, ex, pa=3)
    a_last = acum_e[c - 1:c]
    from_start = jnp.exp2(acum_e)
    to_end = jnp.exp2(a_last - acum_e)
    chunk_decay = jnp.exp2(a_last)
    x_dt = xs * dt_e
    lane = _iota((1, p2), 1)
    mask_l = jnp.where(lane < SSM_HEAD_DIM, 1.0, 0.0)
    mask_r = 1.0 - mask_l
    yield

    pairs_per_group = SSM_HEADS // SSM_GROUPS // 2
    ys = []
    for g in range(SSM_GROUPS):
        bg = bm[:, g * nst:(g + 1) * nst]
        cg = cm[:, g * nst:(g + 1) * nst]
        cbm = _dot(cg, bg, dims=NT)
        bgt = bg.T.astype(BF16)
        cgb = cg.astype(BF16)
        for pp in range(pairs_per_group):
            p = g * pairs_per_group + pp
            sl = slice(p * p2, (p + 1) * p2)
            xp = x_dt[:, sl]
            y = None
            for hh, msk in ((2 * p, mask_l), (2 * p + 1, mask_r)):
                seg = acum[:, hh:hh + 1] - acum_t[hh:hh + 1, :]
                m = cbm * jnp.where(causal, jnp.exp2(jnp.minimum(seg, 0.0)), 0.0)
                t = _dot(m, xp * msk)
                y = t if y is None else y + t
            prev = st_ref[p]
            y = y + _dot(cgb, prev) * from_start[:, sl]
            st_ref[p] = prev * chunk_decay[:, sl] + _dot(bgt, xp * to_end[:, sl])
            ys.append(y + xs[:, sl] * dsk_ref[:, sl])
            yield
    y = jnp.concatenate(ys, axis=1) * _silu(z_ref[...].astype(F32))
    gw = SSM_WIDTH // SSM_GROUPS
    nw = nw_ref[...]
    for g in range(SSM_GROUPS):
        sl = slice(g * gw, (g + 1) * gw)
        yg = y[:, sl]
        ms = jnp.mean(yg * yg, axis=-1, keepdims=True)
        o_ref[:, sl] = (yg * lax.rsqrt(ms + SSM_NORM_EPS) * nw[:, sl]).astype(BF16)
        yield


def _rwkv_body(has_vres, *refs):
    if has_vres:
        (r_ref, k_ref, v_ref, xg_ref, xw_ref, xa_ref, pv_ref, vf_ref,
         mu_rkv_ref, mu_g_ref, mu_w_ref, mu_a_ref, mu_v_ref,
         w0_ref, w2_ref, a0_ref, a2_ref, v0_ref, v2_ref, g2_ref, kk_ref, ka_ref, rk_ref,
         lnw_ref, lnb_ref, o_ref, st_ref, tail_ref) = refs
    else:
        (r_ref, k_ref, v_ref, xg_ref, xw_ref, xa_ref,
         mu_rkv_ref, mu_g_ref, mu_w_ref, mu_a_ref,
         w0_ref, w2_ref, a0_ref, a2_ref, g2_ref, kk_ref, ka_ref, rk_ref,
         lnw_ref, lnb_ref, o_ref, vo_ref, st_ref, tail_ref) = refs
    c, w, p2 = RWKV_CHUNK, RWKV_WIDTH, 2 * RWKV_HEAD_DIM
    rows = o_ref.shape[0]
    nsub = rows // c

    @pl.when(pl.program_id(1) == 0)
    def _():
        st_ref[...] = jnp.zeros_like(st_ref)
        tail_ref[...] = jnp.zeros_like(tail_ref)

    yield
    def mixed(ref, mu, off):
        x = ref[...].astype(F32)
        wd = x.shape[1]
        prev = _shift_rows(x, 1, tail_ref[:, off:off + wd])
        tail_ref[:, off:off + wd] = x[rows - 8:]
        return x + (prev - x) * mu

    mu_rkv = mu_rkv_ref[...]
    r = mixed(r_ref, mu_rkv[:, :w], 0)
    k = mixed(k_ref, mu_rkv[:, w:2 * w], w)
    v = mixed(v_ref, mu_rkv[:, 2 * w:], 2 * w)
    xg = mixed(xg_ref, mu_g_ref[...], 3 * w)
    xw = mixed(xw_ref, mu_w_ref[...], 3 * w + 256)
    xa = mixed(xa_ref, mu_a_ref[...], 3 * w + 384)
    yield

    w_log = -_softplus(-(w0_ref[...] + _dot(jnp.tanh(xw), w2_ref[...], pa=3, pb=3))) - 0.5
    lw = -LOG2E * jnp.exp(w_log)
    gate = _sigmoid(a0_ref[...] + _dot(xa, a2_ref[...]))
    g = _dot(_sigmoid(xg), g2_ref[...])
    yield
    if has_vres:
        pv = mixed(pv_ref, mu_v_ref[...], 3 * w + 512)
        v = v + (vf_ref[...] - v) * _sigmoid(v0_ref[...] + _dot(pv, v2_ref[...]))
    else:
        vo_ref[...] = v

    lane = _iota((1, p2), 1)
    mask_l = jnp.where(lane < RWKV_HEAD_DIM, 1.0, 0.0)
    mask_r = 1.0 - mask_l
    r2 = _iota((p2, p2), 0)
    c2 = _iota((p2, p2), 1)
    same = (r2 < RWKV_HEAD_DIM) == (c2 < RWKV_HEAD_DIM)
    bd_ones = jnp.where(same, 1.0, 0.0).astype(BF16)

    def head_sum(x, pa):
        return jnp.concatenate(
            [_dot(x[:, i * p2:(i + 1) * p2], bd_ones, pa=pa) for i in range(w // p2)], axis=1)

    kk = k * kk_ref[...]
    kk = kk * jnp.minimum(lax.rsqrt(head_sum(kk * kk, 1)), 1e12)
    k = k * (1.0 + (gate - 1.0) * ka_ref[...])
    kb = kk * gate
    yield

    ri = _iota((rows, rows), 0)
    ci = _iota((rows, rows), 1)
    in_chunk = (ri >= ci) & ((ri - ci) <= (ri & (c - 1)))
    tri = jnp.where(in_chunk, 1.0, 0.0).astype(BF16)
    cum = _dot(tri, lw, pb=2)
    c_ends = [cum[(j + 1) * c - 1:(j + 1) * c] for j in range(nsub)]
    c_last = jnp.concatenate([jnp.broadcast_to(x, (c, w)) for x in c_ends], axis=0)
    g_t = jnp.exp2(cum)
    g_inv = jnp.exp2(-cum)
    g_end = jnp.exp2(c_last - cum)
    g_all = [jnp.exp2(x) for x in c_ends]
    a_s = -kk * jnp.exp2(cum - lw)
    r_s = r * g_t
    b_h = kb * g_inv
    k_h = k * g_inv
    b_e = kb * g_end
    k_e = k * g_end
    yield

    lower = (r2 >= c2) & same
    strict = (r2 > c2) & same
    eye = jnp.where(r2 == c2, 1.0, 0.0)

    def stack(x):
        return jnp.concatenate([x * mask_l, x * mask_r], axis=0)

    def twice(x):
        return jnp.concatenate([x, x], axis=0)

    def bf(x):
        return x.astype(BF16)

    def mm(a, b):
        return _mm(a, b, ((1,), (0,)))

    npair = w // p2
    pairs = range(nsub * npair)
    sls = [(slice((q // npair) * c, (q // npair + 1) * c), slice((q % npair) * p2, (q % npair + 1) * p2))
           for q in pairs]
    sa = [bf(stack(a_s[s])) for s in sls]
    sr = [stack(r_s[s]) for s in sls]
    sv = [bf(stack(v[s])) for s in sls]
    yield
    gram = [_mm(jnp.concatenate([sa[p], bf(sr[p])], axis=0),
                bf(jnp.concatenate([twice(b_h[sls[p]]), twice(k_h[sls[p]])], axis=0)), NT)
            for p in pairs]
    n = [jnp.where(strict, x[:p2, :p2], 0.0) for x in gram]
    a_ak = [bf(jnp.where(strict, x[:p2, p2:], 0.0)) for x in gram]
    a_rb = [bf(jnp.where(lower, x[p2:, :p2], 0.0)) for x in gram]
    a_rk = [bf(jnp.where(lower, x[p2:, p2:], 0.0)) for x in gram]
    yield
    t = [eye + x for x in n]
    pw = [bf(x) for x in n]
    for _ in range(5):
        pw = [bf(mm(x, x)) for x in pw]
        t = [t[p] + mm(bf(t[p]), pw[p]) for p in pairs]
        yield
    av = [bf(mm(a_ak[p], sv[p])) for p in pairs]
    yield
    au = [bf(mm(bf(t[p]), jnp.concatenate([sa[p], av[p]], axis=1))) for p in pairs]
    yield
    zero = jnp.zeros((p2, p2), BF16)
    z = [mm(jnp.concatenate([jnp.concatenate([a_rb[p], a_rk[p]], axis=1),
                             jnp.concatenate([bf(stack(b_e[sls[p]]).T), bf(stack(k_e[sls[p]]).T)], axis=1)],
                            axis=0),
            jnp.concatenate([au[p], jnp.concatenate([zero, sv[p]], axis=1)], axis=0))
         for p in pairs]
    lhs = [bf(jnp.concatenate([sr[p] + z[p][:p2, :p2], z[p][p2:, :p2]], axis=0)) for p in pairs]
    yield
    g_col = [jnp.sum(eye * g_all[q // npair][:, sls[q][1]], axis=1, keepdims=True) for q in pairs]
    ys = [[None] * npair for _ in range(nsub)]
    for p in range(npair):
        h = st_ref[p]
        for j in range(nsub):
            q = j * npair + p
            hi, lo = _split(h, 2)
            out = mm(jnp.concatenate([lhs[q], lhs[q]], axis=1), jnp.concatenate([hi, lo], axis=0)) + z[q][:, p2:]
            h = out[p2:] + g_col[q] * h
            ys[j][p] = out[:c] + out[c:p2]
        st_ref[p] = h
    yield
    y = jnp.concatenate([jnp.concatenate(row, axis=1) for row in ys], axis=0)

    mean = head_sum(y, 1) * (1.0 / RWKV_HEAD_DIM)
    yc = y - mean
    var = head_sum(yc * yc, 1) * (1.0 / RWKV_HEAD_DIM)
    yn = yc * lax.rsqrt(var + RWKV_LN_EPS) * lnw_ref[...] + lnb_ref[...]
    bonus = head_sum(r * k * rk_ref[...], 1) * v
    o_ref[...] = ((yn + bonus) * g).astype(BF16)
    yield


def _interleave(gens, lead):
    for g in gens:
        next(g)
    for _ in range(lead):
        next(gens[0])
    live = list(gens)
    while live:
        for g in list(live):
            try:
                next(g)
            except StopIteration:
                live.remove(g)


N_RET_IN, N_SSD_IN = 6, 10


def _mixers_kernel(has_vres, n_rw_in, *refs):
    ret_in = refs[:N_RET_IN]
    ssd_in = refs[N_RET_IN:N_RET_IN + N_SSD_IN]
    rw_in = refs[N_RET_IN + N_SSD_IN:N_RET_IN + N_SSD_IN + n_rw_in]
    rest = refs[N_RET_IN + N_SSD_IN + n_rw_in:]
    n_out = 3 if has_vres else 4
    y_ret, y_ssm = rest[0], rest[1]
    rw_out = rest[2:n_out]
    ret_st, ssd_st, ssd_tail, rw_st, rw_tail = rest[n_out:]
    _interleave([_rwkv_body(has_vres, *rw_in, *rw_out, rw_st, rw_tail),
                 _ssd_body(*ssd_in, y_ssm, ssd_st, ssd_tail),
                 _retention_body(*ret_in, y_ret, ret_st)], lead=5)


def _mixer_params(ssm_conv_w, ssm_conv_b, ssm_dt_bias, ssm_a_log, ssm_d, ssm_norm_w, rwkv_mu, rwkv_mu_v, rwkv_w0,
                  rwkv_w2, rwkv_a0, rwkv_a2, rwkv_v0, rwkv_v2, rwkv_g2, rwkv_k_k, rwkv_k_a, rwkv_r_k, rwkv_ln_w,
                  rwkv_ln_b):
    w = RWKV_WIDTH
    o1 = 3 * w + RWKV_DECAY_RANK
    o2 = o1 + RWKV_AAA_RANK

    def vec(x, n=None):
        n = x.shape[-1] if n is None else n
        return jnp.pad(x, ((0, 0), (0, n - x.shape[-1])))[:, None, :]

    def mat(x, r):
        return jnp.pad(x, ((0, 0), (0, r - x.shape[1]), (0, 0))).astype(BF16)

    ssd = [jnp.pad(ssm_conv_w, ((0, 0), (0, 8 - SSM_CONV), (0, 0))), vec(ssm_conv_b), vec(ssm_dt_bias, LANE),
           vec(ssm_a_log, LANE), vec(jnp.repeat(ssm_d, SSM_HEAD_DIM, axis=1)), vec(ssm_norm_w)]
    mus = [vec(rwkv_mu[:, :3 * w]), vec(rwkv_mu[:, o2:]), vec(rwkv_mu[:, 3 * w:o1], LANE),
           vec(rwkv_mu[:, o1:o2], LANE)]
    low_rank = [vec(rwkv_w0), mat(rwkv_w2, LANE), vec(rwkv_a0), mat(rwkv_a2, LANE)]
    rest = [rwkv_g2.astype(BF16), vec(rwkv_k_k), vec(rwkv_k_a), vec(rwkv_r_k.reshape(-1, w)), vec(rwkv_ln_w),
            vec(rwkv_ln_b)]
    vres = [vec(rwkv_mu_v, LANE), vec(rwkv_v0), mat(rwkv_v2, LANE)]
    expand = (jnp.arange(LANE)[:, None] == (jnp.arange(SSM_WIDTH)[None, :] // SSM_HEAD_DIM)).astype(BF16)
    return dict(ssd=ssd, mus=mus, low_rank=low_rank, rest=rest, vres=vres, expand=expand)


def _mixers(proj, cos, sin, prm, layer, v_first, b, s):
    rows = RWKV_ROWS
    assert rows == RET_CHUNK == SSM_CHUNK
    nc = s // rows
    w = RWKV_WIDTH
    has_vres = v_first is not None

    def row(wd, col):
        return pl.BlockSpec((rows, wd), lambda bi, n, j=col // wd: (bi * nc + n, j))

    def lay(x, idx):
        return pl.BlockSpec((None,) + x.shape[1:], lambda bi, n: (idx, 0, 0))

    tab = pl.BlockSpec((rows, LANE), lambda bi, n: (n, 0))
    ret_args = [proj, proj, proj, proj, cos, sin]
    ret_specs = [row(RET_WIDTH, C_RET + i * RET_WIDTH) for i in range(4)] + [tab, tab]

    ssd_args = [proj, proj, proj] + prm["ssd"] + [prm["expand"]]
    ssd_specs = ([row(SSM_WIDTH, C_Z), row(SSM_CONV_DIM, C_XBC), row(LANE, C_DT)]
                 + [lay(x, layer) for x in prm["ssd"]] + [pl.BlockSpec(prm["expand"].shape, lambda bi, n: (0, 0))])
    assert len(ret_args) == N_RET_IN and len(ssd_args) == N_SSD_IN

    rw_acts = [proj] * 6
    rw_act_specs = [row(w, C_RKV), row(w, C_RKV + w), row(w, C_RKV + 2 * w), row(256, C_XG),
                    row(LANE, C_XW), row(LANE, C_XA)]
    mus = [(x, layer) for x in prm["mus"]]
    par = [(x, layer) for x in prm["low_rank"]]
    if has_vres:
        mu_v, v0, v2 = prm["vres"]
        rw_acts += [proj, v_first]
        rw_act_specs += [row(LANE, C_PV), pl.BlockSpec((rows, w), lambda bi, n: (bi * nc + n, 0))]
        mus.append((mu_v, layer - 1))
        par += [(v0, layer - 1), (v2, layer - 1)]
    par += [(x, layer) for x in prm["rest"]]
    rw_consts = mus + par
    rw_args = rw_acts + [x for x, _ in rw_consts]
    rw_specs = rw_act_specs + [lay(x, idx) for x, idx in rw_consts]

    def out_block(wd):
        return pl.BlockSpec((rows, wd), lambda bi, n: (bi * nc + n, 0))

    t = b * s
    out_shape = [jax.ShapeDtypeStruct((t, RET_WIDTH), BF16), jax.ShapeDtypeStruct((t, SSM_WIDTH), BF16),
                 jax.ShapeDtypeStruct((t, w), BF16)]
    out_specs = [out_block(RET_WIDTH), out_block(SSM_WIDTH), out_block(w)]
    if not has_vres:
        out_shape.append(jax.ShapeDtypeStruct((t, w), F32))
        out_specs.append(out_block(w))
    return pl.pallas_call(
        functools.partial(_mixers_kernel, has_vres, len(rw_args)),
        out_shape=tuple(out_shape),
        grid=(b, nc),
        in_specs=ret_specs + ssd_specs + rw_specs,
        out_specs=tuple(out_specs),
        scratch_shapes=[pltpu.VMEM((RET_HEADS, RET_HEAD_DIM, RET_HEAD_DIM), F32),
                        pltpu.VMEM((SSM_HEADS // 2, SSM_STATE, 2 * SSM_HEAD_DIM), F32),
                        pltpu.VMEM((8, SSM_CONV_DIM), F32),
                        pltpu.VMEM((RWKV_HEADS // 2, 2 * RWKV_HEAD_DIM, 2 * RWKV_HEAD_DIM), F32),
                        pltpu.VMEM((8, 3 * w + 256 + 3 * LANE), F32)],
        compiler_params=_cparams(("parallel", "arbitrary")),
        name="mixers",
    )(*ret_args, *ssd_args, *rw_args)


def _out_proj_kernel(x_ref, yr_ref, ys_ref, yw_ref, w_ref, o_ref):
    r0, r1 = RET_WIDTH, RET_WIDTH + SSM_WIDTH
    acc = x_ref[...]
    acc = acc + jnp.dot(yr_ref[...], w_ref[:r0], preferred_element_type=F32)
    acc = acc + jnp.dot(ys_ref[...], w_ref[r0:r1], preferred_element_type=F32)
    acc = acc + jnp.dot(yw_ref[...], w_ref[r1:], preferred_element_type=F32)
    o_ref[...] = acc


def _out_proj(x, y_ret, y_ssm, y_rwkv, w_out_all, layer, tm=512):
    t, d = x.shape

    def rows(wd):
        return pl.BlockSpec((tm, wd), lambda i: (i, 0))

    return pl.pallas_call(
        _out_proj_kernel,
        out_shape=jax.ShapeDtypeStruct((t, d), F32),
        grid=(t // tm,),
        in_specs=[rows(d), rows(RET_WIDTH), rows(SSM_WIDTH), rows(RWKV_WIDTH),
                  pl.BlockSpec((None,) + w_out_all.shape[1:], lambda i: (layer, 0, 0))],
        out_specs=rows(d),
        compiler_params=_cparams(("parallel",)),
        name="out_proj",
    )(x, y_ret, y_ssm, y_rwkv, w_out_all)


def _ffn_kernel(has_final, *refs):
    if has_final:
        x_ref, nw_ref, wg_ref, wu_ref, wd_ref, fw_ref, o_ref, h_ref = refs
    else:
        x_ref, nw_ref, wg_ref, wu_ref, wd_ref, o_ref, h_ref = refs
    j = pl.program_id(1)

    @pl.when(j == 0)
    def _():
        x = x_ref[...]
        ms = jnp.mean(x * x, axis=-1, keepdims=True)
        h_ref[...] = (x * lax.rsqrt(ms + NORM_EPS) * nw_ref[...]).astype(BF16)
        o_ref[...] = x

    h = h_ref[...]
    gt = jnp.dot(h, wg_ref[...], preferred_element_type=F32)
    up = jnp.dot(h, wu_ref[...], preferred_element_type=F32)
    act = (_silu(gt) * up).astype(BF16)
    o_ref[...] += jnp.dot(act, wd_ref[...], preferred_element_type=F32)

    if has_final:
        @pl.when(j == pl.num_programs(1) - 1)
        def _():
            y = o_ref[...]
            ms = jnp.mean(y * y, axis=-1, keepdims=True)
            o_ref[...] = y * lax.rsqrt(ms + NORM_EPS) * fw_ref[...]


def _ffn(x, nw, wg_all, wu_all, wd_all, layer, final_w=None, tm=1024, tf=512):
    t, d = x.shape
    f = wg_all.shape[2]
    has_final = final_w is not None
    in_specs = [pl.BlockSpec((tm, d), lambda i, j: (i, 0)),
                pl.BlockSpec((1, d), lambda i, j: (0, 0)),
                pl.BlockSpec((None, d, tf), lambda i, j: (layer, 0, j)),
                pl.BlockSpec((None, d, tf), lambda i, j: (layer, 0, j)),
                pl.BlockSpec((None, tf, d), lambda i, j: (layer, j, 0))]
    args = [x, nw.reshape(1, d), wg_all, wu_all, wd_all]
    if has_final:
        in_specs.append(pl.BlockSpec((1, d), lambda i, j: (0, 0)))
        args.append(final_w.reshape(1, d))
    return pl.pallas_call(
        functools.partial(_ffn_kernel, has_final),
        out_shape=jax.ShapeDtypeStruct((t, d), F32),
        grid=(t // tm, f // tf),
        in_specs=in_specs,
        out_specs=pl.BlockSpec((tm, d), lambda i, j: (i, 0)),
        scratch_shapes=[pltpu.VMEM((tm, d), BF16)],
        compiler_params=_cparams(("parallel", "arbitrary")),
        name="ffn",
    )(*args)


def _pack_kernel(w_ref, o_ref):
    rb, n_in = w_ref.shape
    o_z = RET_COLS
    o_xbc = o_z + SSM_WIDTH
    o_dt = o_xbc + SSM_CONV_DIM
    o_rkv = RET_COLS + SSM_COLS
    o_xw = o_rkv + 3 * RWKV_WIDTH
    o_xa = o_xw + RWKV_DECAY_RANK
    o_xg = o_xa + RWKV_AAA_RANK
    o_pv = IN_COLS
    o_ref[:, :o_dt] = w_ref[:, :o_dt]
    moves = [(C_RKV, o_rkv, o_xw), (C_XG, o_xg, o_pv), (C_DT, o_dt, o_rkv), (C_XW, o_xw, o_xa), (C_XA, o_xa, o_xg),
             (C_PV, o_pv, n_in)]
    for dst, lo, hi in moves:
        n = hi - lo
        width = -(-max(n, 1) // LANE) * LANE
        if n > 0:
            lo_al = lo // LANE * LANE
            hi_al = min(-(-hi // LANE) * LANE, n_in)
            win = w_ref[:, lo_al:hi_al].astype(F32)
            o_ref[:, dst:dst + n] = win[:, lo - lo_al:hi - lo_al].astype(BF16)
        if width > n:
            o_ref[:, dst + n:dst + width] = jnp.zeros((rb, width - n), BF16)


def _pack_w_in(w_in_all, layer, rb=256):
    _, d, n_in = w_in_all.shape
    return pl.pallas_call(
        _pack_kernel,
        out_shape=jax.ShapeDtypeStruct((d, IN_PAD), BF16),
        grid=(d // rb,),
        in_specs=[pl.BlockSpec((None, rb, n_in), lambda i: (layer, i, 0))],
        out_specs=pl.BlockSpec((rb, IN_PAD), lambda i: (i, 0)),
        compiler_params=_cparams(("parallel",)),
        name="pack_w_in",
    )(w_in_all)


def kernel(x, norm_mix_w, w_in_first, w_in_rest, ssm_conv_w, ssm_conv_b, ssm_dt_bias, ssm_a_log, ssm_d, ssm_norm_w, rwkv_mu, rwkv_mu_v, rwkv_w0, rwkv_w2, rwkv_a0, rwkv_a2, rwkv_v0, rwkv_v2, rwkv_g2, rwkv_k_k, rwkv_k_a, rwkv_r_k, rwkv_ln_w, rwkv_ln_b, w_out, norm_ffn_w, ffn_w_gate, ffn_w_up, ffn_w_down, final_norm_w):
    b, s, d = x.shape
    depth = norm_mix_w.shape[0]
    xf = x.reshape(b * s, d)
    cos, sin = _rope_tables(s)
    wo_all, wg_all, wu_all, wd_all = (w.astype(BF16) for w in (w_out, ffn_w_gate, ffn_w_up, ffn_w_down))
    w_first_bf, w_rest_bf = w_in_first[None].astype(BF16), w_in_rest.astype(BF16)
    prm = _mixer_params(ssm_conv_w, ssm_conv_b, ssm_dt_bias, ssm_a_log, ssm_d, ssm_norm_w, rwkv_mu, rwkv_mu_v,
                        rwkv_w0, rwkv_w2, rwkv_a0, rwkv_a2, rwkv_v0, rwkv_v2, rwkv_g2, rwkv_k_k, rwkv_k_a, rwkv_r_k,
                        rwkv_ln_w, rwkv_ln_b)
    v_first = None
    for l in range(depth):
        w_packed = _pack_w_in(w_first_bf, 0) if l == 0 else _pack_w_in(w_rest_bf, l - 1)
        proj = _norm_matmul(xf, norm_mix_w[l], w_packed)
        out = _mixers(proj, cos, sin, prm, l, v_first, b, s)
        y_ret, y_ssm, y_rwkv = out[:3]
        if l == 0:
            v_first = out[3]
        xf = _out_proj(xf, y_ret, y_ssm, y_rwkv, wo_all, l)
        xf = _ffn(xf, norm_ffn_w[l], wg_all, wu_all, wd_all, l, final_norm_w if l == depth - 1 else None)
    return xf.reshape(b, s, d)
```
